```python
import math
import jax, jax.numpy as jnp
from jax import lax
import numpy as np


D_MODEL = 1024
BATCH = 2
SEQ = 8192
DEPTH = 1

CHUNK = 64
N_META = 16
Q_BLOCK = 128
LEAD = Q_BLOCK * ((N_META + Q_BLOCK - 1) // Q_BLOCK)
PAD = LEAD - N_META

MIX = D_MODEL
D_MLSTM = MIX // 2
D_SB = MIX - D_MLSTM
MLSTM_HEADS = 4
MLSTM_DH = D_MLSTM // MLSTM_HEADS
SB_HEADS = 8
SB_DH = D_SB // SB_HEADS
CONV_W = 4
FORGET_BIAS_LO = 3.0
FORGET_BIAS_HI = 6.0

PEER_HEADS = 8
N_KEYS = 128
N_EXPERTS = N_KEYS * N_KEYS
PEER_TOPK = 16
PEER_QDIM = 256
PEER_HALF = PEER_QDIM // 2
TOK_BLOCK = 128

EPS = 1e-6
D_IN_PROJ = 3 * D_MLSTM + 2 * MLSTM_HEADS + 3 * D_SB

kernel_name = 'hybrid_mlstm_stickbreak_peer'


def _split_points():
    sizes = (D_MLSTM, D_MLSTM, D_MLSTM, MLSTM_HEADS, MLSTM_HEADS, D_SB, D_SB, D_SB)
    return [int(s) for s in np.cumsum(sizes)[:-1]]


def rmsnorm(x, g):
    xf = x.astype(jnp.float32)
    y = xf * lax.rsqrt(jnp.mean(xf * xf, axis=-1, keepdims=True) + EPS)
    return (y * g.astype(jnp.float32)).astype(x.dtype)


def mlstm_mixer(u, v, o_pre, i_pre, f_pre, conv_w, conv_b, w_q, w_k, b_i, b_f, skip, gn_g, valid):
    B, L, _ = u.shape
    H, dh = MLSTM_HEADS, MLSTM_DH
    f32 = jnp.float32
    u = jnp.where(valid[None, :, None], u, jnp.zeros_like(u))
    c = lax.conv_general_dilated(u, conv_w[:, None, :].astype(u.dtype), window_strides=(1,),
                                 padding=[(CONV_W - 1, 0)], dimension_numbers=('NWC', 'WIO', 'NWC'),
                                 feature_group_count=D_MLSTM) + conv_b
    c = jax.nn.silu(c)
    ch = c.reshape(B, L, H, dh)
    q = jnp.einsum('blhd,hde->blhe', ch, w_q).astype(f32)
    k = (jnp.einsum('blhd,hde->blhe', ch, w_k) * (dh ** -0.5)).astype(f32)
    vv = v.reshape(B, L, H, dh).astype(f32)
    vmask = valid[None, :, None]
    logi = jnp.where(vmask, (i_pre + b_i).astype(f32), -jnp.inf)
    logf = jnp.where(vmask, jax.nn.log_sigmoid((f_pre + b_f).astype(f32)), 0.0)

    NC = L // CHUNK
    def chunk_vec(t):
        return t.reshape(B, NC, CHUNK, H, dh).transpose(1, 0, 3, 2, 4)
    def chunk_gate(t):
        return t.reshape(B, NC, CHUNK, H).transpose(1, 0, 3, 2)
    tril = jnp.tril(jnp.ones((CHUNK, CHUNK), dtype=bool))

    def step(carry, xs):
        C, n, m = carry
        qc, kc, vc, li, lf = xs
        b = jnp.cumsum(lf, axis=-1)
        D = b[..., :, None] - b[..., None, :] + li[..., None, :]
        D = jnp.where(tril, D, -jnp.inf)
        inter = b + m[..., None]
        m_t = jnp.maximum(inter, jnp.max(D, axis=-1))
        w_intra = jnp.exp(D - m_t[..., None])
        w_inter = jnp.exp(inter - m_t)
        s = jnp.einsum('bhtd,bhsd->bhts', qc, kc) * w_intra
        num = w_inter[..., None] * jnp.einsum('bhed,bhtd->bhte', C, qc) + jnp.einsum('bhts,bhse->bhte', s, vc)
        den = w_inter * jnp.einsum('bhd,bhtd->bht', n, qc) + jnp.sum(s, axis=-1)
        h = num / jnp.maximum(jnp.abs(den), jnp.exp(-m_t))[..., None]
        bL = b[..., -1]
        g = bL[..., None] - b + li
        m_new = jnp.maximum(bL + m, jnp.max(g, axis=-1))
        decay = jnp.exp(bL + m - m_new)
        w = jnp.exp(g - m_new[..., None])
        C_new = decay[..., None, None] * C + jnp.einsum('bhs,bhse,bhsd->bhed', w, vc, kc)
        n_new = decay[..., None] * n + jnp.einsum('bhs,bhsd->bhd', w, kc)
        return (C_new, n_new, m_new), h

    init = (jnp.zeros((B, H, dh, dh), f32), jnp.zeros((B, H, dh), f32), jnp.zeros((B, H), f32))
    _, hs = lax.scan(step, init, (chunk_vec(q), chunk_vec(k), chunk_vec(vv), chunk_gate(logi), chunk_gate(logf)))
    h = hs.transpose(1, 0, 3, 2, 4).reshape(B, L, H, dh)
    h = h * jax.nn.sigmoid(o_pre.astype(f32)).reshape(B, L, H, dh)
    mu = jnp.mean(h, axis=-1, keepdims=True)
    var = jnp.mean(jnp.square(h - mu), axis=-1, keepdims=True)
    hn = ((h - mu) * lax.rsqrt(var + EPS)).reshape(B, L, D_MLSTM)
    out = hn * gn_g.astype(f32) + skip.astype(f32) * c.astype(f32)
    return out.astype(u.dtype)


def stick_breaking_mixer(q, k, v, norm_g, valid):
    B, L, _ = q.shape
    H, dh = SB_HEADS, SB_DH
    f32 = jnp.float32
    qh = (q.reshape(B, L, H, dh).transpose(0, 2, 1, 3) * (dh ** -0.5)).astype(f32)
    kh = k.reshape(B, L, H, dh).transpose(0, 2, 1, 3).astype(f32)
    vh = v.reshape(B, L, H, dh).transpose(0, 2, 1, 3).astype(f32)
    spos = jnp.arange(L)

    def block(bi):
        qb = lax.dynamic_slice_in_dim(qh, bi * Q_BLOCK, Q_BLOCK, axis=2)
        z = jnp.einsum('bhtd,bhsd->bhts', qb, kh)
        tpos = bi * Q_BLOCK + jnp.arange(Q_BLOCK)
        mask = (spos[None, :] < tpos[:, None]) & valid[None, :]
        lneg = jnp.where(mask, jax.nn.log_sigmoid(-z), 0.0)
        rexcl = lax.cumsum(lneg, axis=3, reverse=True) - lneg
        A = jnp.where(mask, jnp.exp(jax.nn.log_sigmoid(z) + rexcl), 0.0)
        return jnp.einsum('bhts,bhsd->bhtd', A, vh)

    o = lax.map(block, jnp.arange(L // Q_BLOCK))
    o = o.transpose(1, 2, 0, 3, 4).reshape(B, H, L, dh).transpose(0, 2, 1, 3)
    o = o * lax.rsqrt(jnp.mean(o * o, axis=-1, keepdims=True) + EPS)
    return (o.reshape(B, L, D_SB) * norm_g.astype(f32)).astype(q.dtype)


def peer_ffn(n, w_query, sub_keys, expert_u, expert_v):
    B, L, D = n.shape
    T, Hh, K = TOK_BLOCK, PEER_HEADS, PEER_TOPK
    xb = n.reshape(B * L // T, T, D)

    def block(xt):
        q = (xt @ w_query).reshape(T, Hh, 2, PEER_HALF)
        s = jnp.einsum('thpc,pnc->thpn', q, sub_keys).astype(jnp.float32)
        sv, si = lax.top_k(s, K)
        cand = (sv[:, :, 0, :, None] + sv[:, :, 1, None, :]).reshape(T, Hh, K * K)
        cidx = (si[:, :, 0, :, None] * N_KEYS + si[:, :, 1, None, :]).reshape(T, Hh, K * K)
        fv, fi = lax.top_k(cand, K)
        eidx = jnp.take_along_axis(cidx, fi, axis=-1)
        g = jax.nn.softmax(fv, axis=-1)
        ue = expert_u[eidx]
        ve = expert_v[eidx]
        act = jax.nn.gelu(jnp.einsum('td,thkd->thk', xt, ue).astype(jnp.float32), approximate=False)
        return jnp.einsum('thk,thkd->td', (g * act).astype(xt.dtype), ve)

    return lax.map(block, xb).reshape(B, L, D)


def setup_inputs(seed: int = 0) -> dict:
    key = jax.random.key(seed)
    ks = jax.random.split(key, 20)
    f32 = jnp.float32
    def nrm(k, shape, scale):
        return jax.random.normal(k, shape, f32) * scale
    H, dh = MLSTM_HEADS, MLSTM_DH
    return {
        'x': nrm(ks[0], (BATCH, SEQ, D_MODEL), 1.0),
        'meta': nrm(ks[1], (N_META, D_MODEL), 1.0),
        'norm_mix_g': 1.0 + nrm(ks[2], (DEPTH, D_MODEL), 0.02),
        'w_in': nrm(ks[3], (DEPTH, D_MODEL, D_IN_PROJ), D_MODEL ** -0.5),
        'conv_w': nrm(ks[4], (DEPTH, CONV_W, D_MLSTM), CONV_W ** -0.5),
        'conv_b': nrm(ks[5], (DEPTH, D_MLSTM), 0.02),
        'w_qm': nrm(ks[6], (DEPTH, H, dh, dh), dh ** -0.5),
        'w_km': nrm(ks[7], (DEPTH, H, dh, dh), dh ** -0.5),
        'b_i': nrm(ks[8], (DEPTH, H), 0.1),
        'b_f': jnp.linspace(FORGET_BIAS_LO, FORGET_BIAS_HI, H, dtype=f32)[None, :] + nrm(ks[9], (DEPTH, H), 0.01),
        'mlstm_skip': 1.0 + nrm(ks[10], (DEPTH, D_MLSTM), 0.02),
        'mlstm_gn_g': 1.0 + nrm(ks[11], (DEPTH, D_MLSTM), 0.02),
        'sb_norm_g': 1.0 + nrm(ks[12], (DEPTH, D_SB), 0.02),
        'w_out': nrm(ks[13], (DEPTH, MIX, D_MODEL), MIX ** -0.5),
        'norm_ffn_g': 1.0 + nrm(ks[14], (DEPTH, D_MODEL), 0.02),
        'w_query': nrm(ks[15], (DEPTH, D_MODEL, PEER_HEADS * PEER_QDIM), D_MODEL ** -0.5),
        'sub_keys': nrm(ks[16], (DEPTH, 2, N_KEYS, PEER_HALF), PEER_HALF ** -0.5),
        'expert_u': nrm(ks[17], (DEPTH, N_EXPERTS, D_MODEL), D_MODEL ** -0.5),
        'expert_v': nrm(ks[18], (DEPTH, N_EXPERTS, D_MODEL), 0.5),
        'norm_final_g': 1.0 + nrm(ks[19], (D_MODEL,), 0.02),
    }


def reference(x, meta, norm_mix_g, w_in, conv_w, conv_b, w_qm, w_km, b_i, b_f, mlstm_skip, mlstm_gn_g,
              sb_norm_g, w_out, norm_ffn_g, w_query, sub_keys, expert_u, expert_v, norm_final_g):
    B, S, D = x.shape
    lead = jnp.concatenate([jnp.zeros((B, PAD, D), x.dtype),
                            jnp.broadcast_to(meta.astype(x.dtype)[None], (B, N_META, D))], axis=1)
    h = jnp.concatenate([lead, x], axis=1)
    L = S + LEAD
    valid = jnp.arange(L) >= PAD
    splits = _split_points()
    for l in range(DEPTH):
        n = rmsnorm(h, norm_mix_g[l])
        proj = n @ w_in[l]
        u, vm, o_pre, i_pre, f_pre, qs, ks_, vs = jnp.split(proj, splits, axis=-1)
        ym = mlstm_mixer(u, vm, o_pre, i_pre, f_pre, conv_w[l], conv_b[l], w_qm[l], w_km[l],
                         b_i[l], b_f[l], mlstm_skip[l], mlstm_gn_g[l], valid)
        ys = stick_breaking_mixer(qs, ks_, vs, sb_norm_g[l], valid)
        h = h + jnp.concatenate([ym, ys], axis=-1) @ w_out[l]
        n = rmsnorm(h, norm_ffn_g[l])
        h = h + peer_ffn(n, w_query[l], sub_keys[l], expert_u[l], expert_v[l])
    return rmsnorm(h[:, LEAD:], norm_final_g)
```

```python
import functools

import jax
import jax.numpy as jnp
from jax import lax
from jax.experimental import pallas as pl
from jax.experimental.pallas import tpu as pltpu

F32 = jnp.float32
BF16 = jnp.bfloat16

D_MODEL = 1024
D_MLSTM = 512
D_SB = 512
MLSTM_HEADS = 4
MLSTM_DH = 128
SB_HEADS = 8
SB_DH = 64
CONV_W = 4
N_META = 16
LEAD = 128
PAD = LEAD - N_META
EPS = 1e-6

PEER_HEADS = 8
N_KEYS = 128
PEER_TOPK = 16
PEER_HALF = 128

GATE_LANES = 128
BLK = 128
NEG_INF = float("-inf")
SB_DEAD_LOG = -105.0

VMEM_LIMIT = 56 * 1024 * 1024


def _softplus(z):
    return jnp.maximum(z, 0.0) + jnp.log1p(jnp.exp(-jnp.abs(z)))


def _rms_scale(x):
    return lax.rsqrt(jnp.mean(x * x, axis=-1, keepdims=True) + EPS)


def _inproj_kernel(h_ref, g_ref, wm_ref, wg_ref, u_ref, vm_ref, o_ref, gate_ref, q_ref, k_ref, v_ref):
    x = h_ref[...]
    n = (x * _rms_scale(x) * g_ref[...]).astype(BF16)
    y = jnp.dot(n, wm_ref[...], preferred_element_type=F32)
    u_ref[...] = y[:, 0 * 512:1 * 512]
    vm_ref[...] = y[:, 1 * 512:2 * 512]
    o_ref[...] = y[:, 2 * 512:3 * 512]
    q_ref[...] = y[:, 3 * 512:4 * 512].astype(BF16)
    k_ref[...] = y[:, 4 * 512:5 * 512].astype(BF16)
    v_ref[...] = y[:, 5 * 512:6 * 512].astype(BF16)
    gate_ref[...] = jnp.dot(n, wg_ref[...], preferred_element_type=F32)


def _inproj(h2d, g, w_main, w_gate, tm=256):
    m = h2d.shape[0]
    row = lambda i: (i, 0)
    const = lambda i: (0, 0)
    o512 = pl.BlockSpec((tm, 512), row)
    return pl.pallas_call(
        _inproj_kernel,
        grid=(m // tm,),
        in_specs=[pl.BlockSpec((tm, D_MODEL), row), pl.BlockSpec((1, D_MODEL), const),
                  pl.BlockSpec(w_main.shape, const), pl.BlockSpec(w_gate.shape, const)],
        out_specs=[o512, o512, o512, pl.BlockSpec((tm, GATE_LANES), row), o512, o512, o512],
        out_shape=[jax.ShapeDtypeStruct((m, 512), F32)] * 3
        + [jax.ShapeDtypeStruct((m, GATE_LANES), F32)]
        + [jax.ShapeDtypeStruct((m, 512), BF16)] * 3,
        compiler_params=pltpu.CompilerParams(dimension_semantics=("parallel",),
                                             vmem_limit_bytes=VMEM_LIMIT),
        name="inproj",
    )(h2d, g, w_main, w_gate)


def _mlstm_kernel(u_ref, vm_ref, o_ref, gate_ref, cw_ref, cb_ref, wq_ref, wk_ref, gb_ref, skip_ref,
                  gn_ref, y_ref, c_state, n_state, m_state, u_tail):
    ci = pl.program_id(1)

    @pl.when(ci == 0)
    def _():
        c_state[...] = jnp.zeros_like(c_state)
        n_state[...] = jnp.zeros_like(n_state)
        m_state[...] = jnp.zeros_like(m_state)
        u_tail[...] = jnp.zeros_like(u_tail)

    tpos = ci * BLK + lax.broadcasted_iota(jnp.int32, (BLK, 1), 0)
    valid = tpos >= PAD

    u = jnp.where(valid, u_ref[0], 0.0)
    ext = jnp.concatenate([u_tail[...], u], axis=0)
    u_tail[...] = u[BLK - 8:, :]
    cw = cw_ref[...]
    c = cb_ref[...] + cw[CONV_W - 1:CONV_W, :] * u
    for j in range(CONV_W - 1):
        off = 8 - (CONV_W - 1) + j
        c = c + cw[j:j + 1, :] * ext[off:off + BLK, :]
    c = c * jax.nn.sigmoid(c)
    c16 = c.astype(BF16)

    gpre = gate_ref[0] + gb_ref[...]
    logi = jnp.where(valid, gpre, NEG_INF)
    logf = jnp.where(valid, -_softplus(-gpre), 0.0)
    r = lax.broadcasted_iota(jnp.int32, (BLK, BLK), 0)
    s = lax.broadcasted_iota(jnp.int32, (BLK, BLK), 1)
    tril = s <= r
    bcum = jnp.dot(tril.astype(F32), logf, preferred_element_type=F32,
                   precision=lax.Precision.HIGHEST)
    logi_t = logi.T
    bcum_t = bcum.T

    vm = vm_ref[0]
    outs = []
    for h in range(MLSTM_HEADS):
        hs = slice(h * MLSTM_DH, (h + 1) * MLSTM_DH)
        ch = c16[:, hs]
        qh = jnp.dot(ch, wq_ref[h], preferred_element_type=F32)
        kh = jnp.dot(ch, wk_ref[h], preferred_element_type=F32) * (MLSTM_DH ** -0.5)
        vh = vm[:, hs].astype(BF16)
        q16 = qh.astype(BF16)
        k16 = kh.astype(BF16)

        b_col = bcum[:, MLSTM_HEADS + h:MLSTM_HEADS + h + 1]
        li_col = logi[:, h:h + 1]
        b_row = bcum_t[MLSTM_HEADS + h:MLSTM_HEADS + h + 1, :]
        li_row = logi_t[h:h + 1, :]
        m_prev = m_state[h]
        c_prev = c_state[h]
        n_prev = n_state[h]

        dmat = jnp.where(tril, b_col - b_row + li_row, NEG_INF)
        inter = b_col + m_prev
        m_t = jnp.maximum(inter, jnp.max(dmat, axis=-1, keepdims=True))
        w_intra = jnp.exp(dmat - m_t)
        w_inter = jnp.exp(inter - m_t)
        sc = lax.dot_general(q16, k16, (((1,), (1,)), ((), ())), preferred_element_type=F32) * w_intra
        num = w_inter * jnp.dot(q16, c_prev.astype(BF16), preferred_element_type=F32) \
            + jnp.dot(sc.astype(BF16), vh, preferred_element_type=F32)
        den = w_inter * jnp.sum(qh * n_prev, axis=-1, keepdims=True) + jnp.sum(sc, axis=-1, keepdims=True)
        outs.append(num / jnp.maximum(jnp.abs(den), jnp.exp(-m_t)))

        b_last = b_col[BLK - 1:BLK, :]
        g_row = b_last - b_row + li_row
        g_col = b_last - b_col + li_col
        m_new = jnp.maximum(b_last + m_prev, jnp.max(g_row, axis=-1, keepdims=True))
        decay = jnp.exp(b_last + m_prev - m_new)
        kw = kh * jnp.exp(g_col - m_new)
        c_state[h] = decay * c_prev + jnp.dot(kw.T.astype(BF16), vh, preferred_element_type=F32)
        n_state[h] = decay * n_prev + jnp.sum(kw, axis=0, keepdims=True)
        m_state[h] = m_new

    o_gate = jax.nn.sigmoid(o_ref[0])
    normed = []
    for h in range(MLSTM_HEADS):
        hs = slice(h * MLSTM_DH, (h + 1) * MLSTM_DH)
        hh = outs[h] * o_gate[:, hs]
        mu = jnp.mean(hh, axis=-1, keepdims=True)
        var = jnp.mean(jnp.square(hh - mu), axis=-1, keepdims=True)
        normed.append((hh - mu) * lax.rsqrt(var + EPS))
    hn = jnp.concatenate(normed, axis=-1)
    y_ref[0] = hn * gn_ref[...] + skip_ref[...] * c


def _mlstm(u, vm, o_pre, gates, conv_w, conv_b, wq, wk, gate_bias, skip, gn_g):
    b, l, _ = u.shape
    blk = lambda bi, ci: (bi, ci, 0)
    c2 = lambda bi, ci: (0, 0)
    c3 = lambda bi, ci: (0, 0, 0)
    t512 = pl.BlockSpec((1, BLK, D_MLSTM), blk)
    return pl.pallas_call(
        _mlstm_kernel,
        grid=(b, l // BLK),
        in_specs=[t512, t512, t512, pl.BlockSpec((1, BLK, GATE_LANES), blk),
                  pl.BlockSpec((CONV_W, D_MLSTM), c2), pl.BlockSpec((1, D_MLSTM), c2),
                  pl.BlockSpec(wq.shape, c3), pl.BlockSpec(wk.shape, c3),
                  pl.BlockSpec((1, GATE_LANES), c2), pl.BlockSpec((1, D_MLSTM), c2),
                  pl.BlockSpec((1, D_MLSTM), c2)],
        out_specs=pl.BlockSpec((1, BLK, D_MLSTM), lambda bi, ci: (bi, jnp.maximum(ci - LEAD // BLK, 0), 0)),
        out_shape=jax.ShapeDtypeStruct((b, l - LEAD, D_MLSTM), F32),
        scratch_shapes=[pltpu.VMEM((MLSTM_HEADS, MLSTM_DH, MLSTM_DH), F32),
                        pltpu.VMEM((MLSTM_HEADS, 1, MLSTM_DH), F32),
                        pltpu.VMEM((MLSTM_HEADS, 1, 1), F32),
                        pltpu.VMEM((8, D_MLSTM), F32)],
        compiler_params=pltpu.CompilerParams(dimension_semantics=("parallel", "arbitrary"),
                                             vmem_limit_bytes=VMEM_LIMIT),
        name="mlstm",
    )(u, vm, o_pre, gates, conv_w, conv_b, wq, wk, gate_bias, skip, gn_g)


def _sb_kernel(q_ref, k_ref, v_ref, g_ref, o_ref):
    qblk = pl.program_id(2) + LEAD // BLK
    q16 = (q_ref[0, 0].astype(F32) * (SB_DH ** -0.5)).astype(BF16)
    tpos = qblk * BLK + lax.broadcasted_iota(jnp.int32, (BLK, BLK), 0)
    srel = lax.broadcasted_iota(jnp.int32, (BLK, BLK), 1)
    jj = lax.broadcasted_iota(jnp.int32, (BLK, BLK), 0)
    later = (jj > srel).astype(BF16)

    def cond(carry):
        kb, go, _, _ = carry
        return jnp.logical_and(kb >= 0, go > 0)

    def body(carry):
        kb, _, run, acc = carry
        start = pl.multiple_of(kb * BLK, BLK)
        k16 = k_ref[0, 0, pl.ds(start, BLK), :]
        v16 = v_ref[0, 0, pl.ds(start, BLK), :]
        z = lax.dot_general(q16, k16, (((1,), (1,)), ((), ())), preferred_element_type=F32)
        spos = start + srel
        mask = jnp.logical_and(spos < tpos, spos >= PAD)
        lneg = jnp.where(mask, -_softplus(z), 0.0)
        hi = lneg.astype(BF16)
        lo = (lneg - hi.astype(F32)).astype(BF16)
        rex = jnp.dot(hi, later, preferred_element_type=F32) + jnp.dot(lo, later, preferred_element_type=F32)
        a = jnp.where(mask, jnp.exp(z + lneg + rex + run), 0.0)
        acc = acc + jnp.dot(a.astype(BF16), v16, preferred_element_type=F32)
        run = run + rex[:, 0:1] + lneg[:, 0:1]
        go = (jnp.max(run) > SB_DEAD_LOG).astype(jnp.int32)
        return kb - 1, go, run, acc

    init = (qblk, jnp.int32(1), jnp.zeros((BLK, 1), F32), jnp.zeros((BLK, SB_DH), F32))
    _, _, _, acc = lax.while_loop(cond, body, init)
    o_ref[0, 0] = acc * _rms_scale(acc) * g_ref[0]


def _sb(q, k, v, norm_g):
    b, h, l, dh = q.shape
    nq = (l - LEAD) // BLK
    full = pl.BlockSpec((1, 1, l, dh), lambda bi, hi, qi: (bi, hi, 0, 0))
    return pl.pallas_call(
        _sb_kernel,
        grid=(b, h, nq),
        in_specs=[pl.BlockSpec((1, 1, BLK, dh), lambda bi, hi, qi: (bi, hi, qi + LEAD // BLK, 0)),
                  full, full, pl.BlockSpec((1, 1, dh), lambda bi, hi, qi: (hi, 0, 0))],
        out_specs=pl.BlockSpec((1, 1, BLK, dh), lambda bi, hi, qi: (bi, hi, qi, 0)),
        out_shape=jax.ShapeDtypeStruct((b, h, l - LEAD, dh), F32),
        compiler_params=pltpu.CompilerParams(dimension_semantics=("parallel", "parallel", "arbitrary"),
                                             vmem_limit_bytes=VMEM_LIMIT),
        name="stickbreak",
    )(q, k, v, norm_g)


def _outproj_kernel(x_ref, ym_ref, ys_ref, wm_ref, ws_ref, g_ref, h_ref, n_ref):
    h = x_ref[0] + jnp.dot(ym_ref[0].astype(BF16), wm_ref[...], preferred_element_type=F32) \
        + jnp.dot(ys_ref[0].astype(BF16), ws_ref[...], preferred_element_type=F32)
    h_ref[0] = h
    n_ref[0] = (h * _rms_scale(h) * g_ref[...]).astype(BF16)


def _outproj(x, ym, ys, w_m, w_s, g, tm=512):
    b, s, d = x.shape
    blk = lambda bi, i: (bi, i, 0)
    c2 = lambda bi, i: (0, 0)
    return pl.pallas_call(
        _outproj_kernel,
        grid=(b, s // tm),
        in_specs=[pl.BlockSpec((1, tm, d), blk), pl.BlockSpec((1, tm, D_MLSTM), blk),
                  pl.BlockSpec((1, tm, D_SB), blk),
                  pl.BlockSpec(w_m.shape, c2), pl.BlockSpec(w_s.shape, c2), pl.BlockSpec((1, d), c2)],
        out_specs=[pl.BlockSpec((1, tm, d), blk), pl.BlockSpec((1, tm, d), blk)],
        out_shape=[jax.ShapeDtypeStruct((b, s, d), F32), jax.ShapeDtypeStruct((b, s, d), BF16)],
        compiler_params=pltpu.CompilerParams(dimension_semantics=("parallel", "parallel"),
                                             vmem_limit_bytes=VMEM_LIMIT),
        name="outproj",
    )(x, ym, ys, w_m, w_s, g)


def _top_values(s, k):
    vals = []
    for _ in range(k):
        m = jnp.max(s, axis=0, keepdims=True)
        vals.append(m)
        s = jnp.where(s == m, NEG_INF, s)
    return vals


def _route_kernel(n_ref, wq_ref, keys_ref, s1_ref, s2_ref, tau_ref, rz_ref):
    qt = lax.dot_general(wq_ref[...], n_ref[...], (((1,), (1,)), ((), ())), preferred_element_type=F32)
    for h in range(PEER_HEADS):
        shifted = []
        tops = []
        for p in range(2):
            r0 = (h * 2 + p) * PEER_HALF
            sc = jnp.dot(keys_ref[p], qt[r0:r0 + PEER_HALF, :].astype(BF16), preferred_element_type=F32)
            top = _top_values(sc, PEER_TOPK)
            shifted.append(sc - top[0])
            tops.append([t - top[0] for t in top])
        cand = []
        for r1 in range(PEER_TOPK):
            n2 = PEER_TOPK // (r1 + 1)
            cand.append(tops[0][r1] + jnp.concatenate(tops[1][:n2], axis=0))
        cand = jnp.concatenate(cand, axis=0)
        tau = _top_values(cand, PEER_TOPK)[-1]
        z = jnp.sum(jnp.where(cand >= tau, jnp.exp(cand), 0.0), axis=0, keepdims=True)
        s1_ref[h] = shifted[0]
        s2_ref[h] = shifted[1]
        tau_ref[h] = tau
        rz_ref[h] = 1.0 / z


def _route(n2d, wq_t, keys16, tb=256):
    m = n2d.shape[0]
    big = pl.BlockSpec((PEER_HEADS, N_KEYS, tb), lambda i: (0, 0, i))
    small = pl.BlockSpec((PEER_HEADS, 1, tb), lambda i: (0, 0, i))
    return pl.pallas_call(
        _route_kernel,
        grid=(m // tb,),
        in_specs=[pl.BlockSpec((tb, D_MODEL), lambda i: (i, 0)),
                  pl.BlockSpec(wq_t.shape, lambda i: (0, 0)),
                  pl.BlockSpec(keys16.shape, lambda i: (0, 0, 0))],
        out_specs=[big, big, small, small],
        out_shape=[jax.ShapeDtypeStruct((PEER_HEADS, N_KEYS, m), F32)] * 2
        + [jax.ShapeDtypeStruct((PEER_HEADS, 1, m), F32)] * 2,
        compiler_params=pltpu.CompilerParams(dimension_semantics=("parallel",),
                                             vmem_limit_bytes=VMEM_LIMIT),
        name="peer_route",
    )(n2d, wq_t, keys16)


def _expert_kernel(n_ref, u_ref, vt_ref, s1_ref, s2_ref, tau_ref, rz_ref, h_ref, g_ref, o_ref, acc_ref,
                   wa_ref, *, rows_per_step):
    ei = pl.program_id(1)

    @pl.when(ei == 0)
    def _():
        acc_ref[...] = jnp.zeros_like(acc_ref)

    act = lax.dot_general(u_ref[...], n_ref[...], (((1,), (1,)), ((), ())), preferred_element_type=F32)
    act = 0.5 * act * (1.0 + lax.erf(act * (2.0 ** -0.5)))
    for r in range(rows_per_step):
        gate = None
        for h in range(PEER_HEADS):
            pair = s1_ref[h, r:r + 1, :] + s2_ref[h]
            wgt = jnp.where(pair >= tau_ref[h], jnp.exp(pair), 0.0) * rz_ref[h]
            gate = wgt if gate is None else gate + wgt
        wa_ref[r * N_KEYS:(r + 1) * N_KEYS, :] = (gate * act[r * N_KEYS:(r + 1) * N_KEYS, :]).astype(BF16)
    acc_ref[...] += jnp.dot(vt_ref[...], wa_ref[...], preferred_element_type=F32)

    @pl.when(ei == pl.num_programs(1) - 1)
    def _():
        h = h_ref[...] + acc_ref[...].T
        o_ref[...] = h * _rms_scale(h) * g_ref[...]


def _experts(n2d, u16, vt16, s1, s2, tau, rz, h2d, g, tb=512, eb=1024):
    m = n2d.shape[0]
    ne = u16.shape[0]
    rows = eb // N_KEYS
    tok = lambda ti, ei: (ti, 0)
    return pl.pallas_call(
        functools.partial(_expert_kernel, rows_per_step=rows),
        grid=(m // tb, ne // eb),
        in_specs=[pl.BlockSpec((tb, D_MODEL), tok),
                  pl.BlockSpec((eb, D_MODEL), lambda ti, ei: (ei, 0)),
                  pl.BlockSpec((D_MODEL, eb), lambda ti, ei: (0, ei)),
                  pl.BlockSpec((PEER_HEADS, rows, tb), lambda ti, ei: (0, ei, ti)),
                  pl.BlockSpec((PEER_HEADS, N_KEYS, tb), lambda ti, ei: (0, 0, ti)),
                  pl.BlockSpec((PEER_HEADS, 1, tb), lambda ti, ei: (0, 0, ti)),
                  pl.BlockSpec((PEER_HEADS, 1, tb), lambda ti, ei: (0, 0, ti)),
                  pl.BlockSpec((tb, D_MODEL), tok),
                  pl.BlockSpec((1, D_MODEL), lambda ti, ei: (0, 0))],
        out_specs=pl.BlockSpec((tb, D_MODEL), tok),
        out_shape=jax.ShapeDtypeStruct((m, D_MODEL), F32),
        scratch_shapes=[pltpu.VMEM((D_MODEL, tb), F32), pltpu.VMEM((eb, tb), BF16)],
        compiler_params=pltpu.CompilerParams(dimension_semantics=("parallel", "arbitrary"),
                                             vmem_limit_bytes=VMEM_LIMIT),
        name="peer_experts",
    )(n2d, u16, vt16, s1, s2, tau, rz, h2d, g)


def kernel(x, meta, norm_mix_g, w_in, conv_w, conv_b, w_qm, w_km, b_i, b_f, mlstm_skip, mlstm_gn_g,
           sb_norm_g, w_out, norm_ffn_g, w_query, sub_keys, expert_u, expert_v, norm_final_g):
    b, s, d = x.shape
    depth = w_in.shape[0]
    assert depth == 1 and d == D_MODEL and s % 512 == 0
    l = s + LEAD
    lead = jnp.concatenate([jnp.zeros((PAD, d), x.dtype), meta.astype(x.dtype)], axis=0)
    h = jnp.concatenate([jnp.broadcast_to(lead[None], (b, LEAD, d)), x], axis=1)

    w = w_in[0]
    o_u, o_v, o_o = 0, D_MLSTM, 2 * D_MLSTM
    o_i = 3 * D_MLSTM
    o_f = o_i + MLSTM_HEADS
    o_q = o_f + MLSTM_HEADS
    w_main = jnp.concatenate([w[:, o_u:o_i], w[:, o_q:]], axis=1).astype(BF16)
    w_gate = jnp.pad(w[:, o_i:o_q], ((0, 0), (0, GATE_LANES - 2 * MLSTM_HEADS))).astype(BF16)
    gate_bias = jnp.pad(jnp.concatenate([b_i[0], b_f[0]]), (0, GATE_LANES - 2 * MLSTM_HEADS))[None, :]

    u, vm, o_pre, gates, qs, ks, vs = _inproj(h.reshape(b * l, d), norm_mix_g[0][None, :], w_main, w_gate)

    r3 = lambda t: t.reshape(b, l, -1)
    ym = _mlstm(r3(u), r3(vm), r3(o_pre), r3(gates), conv_w[0], conv_b[0][None, :],
                w_qm[0].astype(BF16), w_km[0].astype(BF16), gate_bias,
                mlstm_skip[0][None, :], mlstm_gn_g[0][None, :])

    heads = lambda t: t.reshape(b, l, SB_HEADS, SB_DH).transpose(0, 2, 1, 3)
    ys = _sb(heads(qs), heads(ks), heads(vs), sb_norm_g[0].reshape(SB_HEADS, 1, SB_DH))
    ys = ys.transpose(0, 2, 1, 3).reshape(b, s, D_SB)

    wo = w_out[0].astype(BF16)
    h2, n2 = _outproj(x, ym, ys, wo[:D_MLSTM], wo[D_MLSTM:], norm_ffn_g[0][None, :])

    n2d = n2.reshape(b * s, d)
    s1, s2, tau, rz = _route(n2d, w_query[0].T.astype(BF16), sub_keys[0].astype(BF16))
    out = _experts(n2d, expert_u[0].astype(BF16), expert_v[0].T.astype(BF16), s1, s2, tau, rz,
                   h2.reshape(b * s, d), norm_final_g[None, :])
    return out.reshape(b, s, d)
```

```python
import functools

import jax
import jax.numpy as jnp
from jax import lax
from jax.experimental import pallas as pl
from jax.experimental.pallas import tpu as pltpu

F32 = jnp.float32
BF16 = jnp.bfloat16

D_MODEL = 1024
D_MLSTM = 512
D_SB = 512
MLSTM_HEADS = 4
MLSTM_DH = 128
SB_HEADS = 8
SB_DH = 64
CONV_W = 4
N_META = 16
LEAD = 128
PAD = LEAD - N_META
EPS = 1e-6

PEER_HEADS = 8
N_KEYS = 128
PEER_TOPK = 16
PEER_HALF = 128

GATE_LANES = 128
BLK = 128
NEG_INF = float("-inf")
SB_DEAD_LOG = -105.0

VMEM_LIMIT = 56 * 1024 * 1024


def _softplus(z):
    return jnp.maximum(z, 0.0) + jnp.log1p(jnp.exp(-jnp.abs(z)))


def _rms_scale(x):
    return lax.rsqrt(jnp.mean(x * x, axis=-1, keepdims=True) + EPS)


def _inproj_kernel(h_ref, g_ref, wm_ref, wg_ref, u_ref, vm_ref, o_ref, gate_ref, q_ref, k_ref, v_ref):
    x = h_ref[...]
    n = (x * _rms_scale(x) * g_ref[...]).astype(BF16)
    y = jnp.dot(n, wm_ref[...], preferred_element_type=F32)
    u_ref[...] = y[:, 0 * 512:1 * 512]
    vm_ref[...] = y[:, 1 * 512:2 * 512]
    o_ref[...] = y[:, 2 * 512:3 * 512]
    q_ref[...] = y[:, 3 * 512:4 * 512].astype(BF16)
    k_ref[...] = y[:, 4 * 512:5 * 512].astype(BF16)
    v_ref[...] = y[:, 5 * 512:6 * 512].astype(BF16)
    gate_ref[...] = jnp.dot(n, wg_ref[...], preferred_element_type=F32)


def _inproj(h2d, g, w_main, w_gate, tm=256):
    m = h2d.shape[0]
    row = lambda i: (i, 0)
    const = lambda i: (0, 0)
    o512 = pl.BlockSpec((tm, 512), row)
    return pl.pallas_call(
        _inproj_kernel,
        grid=(m // tm,),
        in_specs=[pl.BlockSpec((tm, D_MODEL), row), pl.BlockSpec((1, D_MODEL), const),
                  pl.BlockSpec(w_main.shape, const), pl.BlockSpec(w_gate.shape, const)],
        out_specs=[o512, o512, o512, pl.BlockSpec((tm, GATE_LANES), row), o512, o512, o512],
        out_shape=[jax.ShapeDtypeStruct((m, 512), F32)] * 3
        + [jax.ShapeDtypeStruct((m, GATE_LANES), F32)]
        + [jax.ShapeDtypeStruct((m, 512), BF16)] * 3,
        compiler_params=pltpu.CompilerParams(dimension_semantics=("parallel",),
                                             vmem_limit_bytes=VMEM_LIMIT),
        name="inproj",
    )(h2d, g, w_main, w_gate)


def _mlstm_kernel(u_ref, vm_ref, o_ref, gate_ref, cw_ref, cb_ref, wq_ref, wk_ref, gb_ref, skip_ref,
                  gn_ref, y_ref, c_state, n_state, m_state, u_tail):
    ci = pl.program_id(1)

    @pl.when(ci == 0)
    def _():
        c_state[...] = jnp.zeros_like(c_state)
        n_state[...] = jnp.zeros_like(n_state)
        m_state[...] = jnp.zeros_like(m_state)
        u_tail[...] = jnp.zeros_like(u_tail)

    tpos = ci * BLK + lax.broadcasted_iota(jnp.int32, (BLK, 1), 0)
    valid = tpos >= PAD

    u = jnp.where(valid, u_ref[0], 0.0)
    ext = jnp.concatenate([u_tail[...], u], axis=0)
    u_tail[...] = u[BLK - 8:, :]
    cw = cw_ref[...]
    c = cb_ref[...] + cw[CONV_W - 1:CONV_W, :] * u
    for j in range(CONV_W - 1):
        off = 8 - (CONV_W - 1) + j
        c = c + cw[j:j + 1, :] * ext[off:off + BLK, :]
    c = c * jax.nn.sigmoid(c)
    c16 = c.astype(BF16)

    gpre = gate_ref[0] + gb_ref[...]
    logi = jnp.where(valid, gpre, NEG_INF)
    logf = jnp.where(valid, -_softplus(-gpre), 0.0)
    r = lax.broadcasted_iota(jnp.int32, (BLK, BLK), 0)
    s = lax.broadcasted_iota(jnp.int32, (BLK, BLK), 1)
    tril = s <= r
    bcum = jnp.dot(tril.astype(F32), logf, preferred_element_type=F32,
                   precision=lax.Precision.HIGHEST)
    logi_t = logi.T
    bcum_t = bcum.T

    vm = vm_ref[0]
    outs = []
    for h in range(MLSTM_HEADS):
        hs = slice(h * MLSTM_DH, (h + 1) * MLSTM_DH)
        ch = c16[:, hs]
        qh = jnp.dot(ch, wq_ref[h], preferred_element_type=F32)
        kh = jnp.dot(ch, wk_ref[h], preferred_element_type=F32) * (MLSTM_DH ** -0.5)
        vh = vm[:, hs].astype(BF16)
        q16 = qh.astype(BF16)
        k16 = kh.astype(BF16)

        b_col = bcum[:, MLSTM_HEADS + h:MLSTM_HEADS + h + 1]
        li_col = logi[:, h:h + 1]
        b_row = bcum_t[MLSTM_HEADS + h:MLSTM_HEADS + h + 1, :]
        li_row = logi_t[h:h + 1, :]
        m_prev = m_state[h]
        c_prev = c_state[h]
        n_prev = n_state[h]

        dmat = jnp.where(tril, b_col - b_row + li_row, NEG_INF)
        inter = b_col + m_prev
        m_t = jnp.maximum(inter, jnp.max(dmat, axis=-1, keepdims=True))
        w_intra = jnp.exp(dmat - m_t)
        w_inter = jnp.exp(inter - m_t)
        sc = lax.dot_general(q16, k16, (((1,), (1,)), ((), ())), preferred_element_type=F32) * w_intra
        num = w_inter * jnp.dot(q16, c_prev.astype(BF16), preferred_element_type=F32) \
            + jnp.dot(sc.astype(BF16), vh, preferred_element_type=F32)
        den = w_inter * jnp.sum(qh * n_prev, axis=-1, keepdims=True) + jnp.sum(sc, axis=-1, keepdims=True)
        outs.append(num / jnp.maximum(jnp.abs(den), jnp.exp(-m_t)))

        b_last = b_col[BLK - 1:BLK, :]
        g_row = b_last - b_row + li_row
        g_col = b_last - b_col + li_col
        m_new = jnp.maximum(b_last + m_prev, jnp.max(g_row, axis=-1, keepdims=True))
        decay = jnp.exp(b_last + m_prev - m_new)
        kw = kh * jnp.exp(g_col - m_new)
        c_state[h] = decay * c_prev + jnp.dot(kw.T.astype(BF16), vh, preferred_element_type=F32)
        n_state[h] = decay * n_prev + jnp.sum(kw, axis=0, keepdims=True)
        m_state[h] = m_new

    o_gate = jax.nn.sigmoid(o_ref[0])
    normed = []
    for h in range(MLSTM_HEADS):
        hs = slice(h * MLSTM_DH, (h + 1) * MLSTM_DH)
        hh = outs[h] * o_gate[:, hs]
        mu = jnp.mean(hh, axis=-1, keepdims=True)
        var = jnp.mean(jnp.square(hh - mu), axis=-1, keepdims=True)
        normed.append((hh - mu) * lax.rsqrt(var + EPS))
    hn = jnp.concatenate(normed, axis=-1)
    y_ref[0] = hn * gn_ref[...] + skip_ref[...] * c


def _mlstm(u, vm, o_pre, gates, conv_w, conv_b, wq, wk, gate_bias, skip, gn_g):
    b, l, _ = u.shape
    blk = lambda bi, ci: (bi, ci, 0)
    c2 = lambda bi, ci: (0, 0)
    c3 = lambda bi, ci: (0, 0, 0)
    t512 = pl.BlockSpec((1, BLK, D_MLSTM), blk)
    return pl.pallas_call(
        _mlstm_kernel,
        grid=(b, l // BLK),
        in_specs=[t512, t512, t512, pl.BlockSpec((1, BLK, GATE_LANES), blk),
                  pl.BlockSpec((CONV_W, D_MLSTM), c2), pl.BlockSpec((1, D_MLSTM), c2),
                  pl.BlockSpec(wq.shape, c3), pl.BlockSpec(wk.shape, c3),
                  pl.BlockSpec((1, GATE_LANES), c2), pl.BlockSpec((1, D_MLSTM), c2),
                  pl.BlockSpec((1, D_MLSTM), c2)],
        out_specs=pl.BlockSpec((1, BLK, D_MLSTM), lambda bi, ci: (bi, jnp.maximum(ci - LEAD // BLK, 0), 0)),
        out_shape=jax.ShapeDtypeStruct((b, l - LEAD, D_MLSTM), F32),
        scratch_shapes=[pltpu.VMEM((MLSTM_HEADS, MLSTM_DH, MLSTM_DH), F32),
                        pltpu.VMEM((MLSTM_HEADS, 1, MLSTM_DH), F32),
                        pltpu.VMEM((MLSTM_HEADS, 1, 1), F32),
                        pltpu.VMEM((8, D_MLSTM), F32)],
        compiler_params=pltpu.CompilerParams(dimension_semantics=("parallel", "arbitrary"),
                                             vmem_limit_bytes=VMEM_LIMIT),
        name="mlstm",
    )(u, vm, o_pre, gates, conv_w, conv_b, wq, wk, gate_bias, skip, gn_g)


PAIR = 2 * SB_DH
N_PAIRS = SB_HEADS // 2


def _sb_kernel(q_ref, k_ref, v_ref, g_ref, o_ref, run_ref, acc_ref):
    qblk = pl.program_id(1) + LEAD // BLK
    row = lax.broadcasted_iota(jnp.int32, (BLK, BLK), 0)
    col = lax.broadcasted_iota(jnp.int32, (BLK, BLK), 1)
    odd_lane = col >= SB_DH
    jj = lax.broadcasted_iota(jnp.int32, (BLK, 2 * BLK), 0)
    ss = lax.broadcasted_iota(jnp.int32, (BLK, 2 * BLK), 1)
    tri2 = jnp.logical_or(jj > ss, ss >= BLK).astype(BF16)

    qf = q_ref[0].astype(F32) * (SB_DH ** -0.5)
    qm = []
    for h in range(SB_HEADS):
        qp = qf[:, (h // 2) * PAIR:(h // 2 + 1) * PAIR]
        qm.append(jnp.where(odd_lane if h % 2 else jnp.logical_not(odd_lane), qp, 0.0).astype(BF16))

    def key_block(start, mask, first):
        kblk = k_ref[0, pl.ds(start, BLK), :]
        vblk = v_ref[0, pl.ds(start, BLK), :]
        for p in range(N_PAIRS):
            kp = kblk[:, p * PAIR:(p + 1) * PAIR]
            vp = vblk[:, p * PAIR:(p + 1) * PAIR]
            a_pair = []
            for h in (2 * p, 2 * p + 1):
                z = lax.dot_general(qm[h], kp, (((1,), (1,)), ((), ())), preferred_element_type=F32)
                lneg = jnp.where(mask, -_softplus(z), 0.0)
                hi = lneg.astype(BF16)
                lo = (lneg - hi.astype(F32)).astype(BF16)
                rt = jnp.dot(hi, tri2, preferred_element_type=F32) + jnp.dot(lo, tri2, preferred_element_type=F32)
                e = z + lneg + rt[:, :BLK]
                if not first:
                    e = e + run_ref[h]
                a_pair.append(jnp.where(mask, jnp.exp(e), 0.0).astype(BF16))
                run_ref[h] = rt[:, BLK:] if first else run_ref[h] + rt[:, BLK:]
            zero = jnp.zeros_like(vp)
            v_heads = jnp.concatenate([jnp.where(odd_lane, zero, vp), jnp.where(odd_lane, vp, zero)], axis=0)
            pv = jnp.dot(jnp.concatenate(a_pair, axis=1), v_heads, preferred_element_type=F32)
            ps = slice(p * PAIR, (p + 1) * PAIR)
            acc_ref[:, ps] = pv if first else acc_ref[:, ps] + pv

    def alive():
        m = run_ref[0]
        for h in range(1, SB_HEADS):
            m = jnp.maximum(m, run_ref[h])
        return (jnp.max(m) > SB_DEAD_LOG).astype(jnp.int32)

    key_block(pl.multiple_of(qblk * BLK, BLK), col < row, True)

    def cond(carry):
        kb, go = carry
        return jnp.logical_and(kb >= 0, go > 0)

    def body(carry):
        kb, _ = carry
        start = pl.multiple_of(kb * BLK, BLK)
        key_block(start, start + col >= PAD, False)
        return kb - 1, alive()

    lax.while_loop(cond, body, (qblk - 1, alive()))

    for p in range(N_PAIRS):
        ps = slice(p * PAIR, (p + 1) * PAIR)
        x = acc_ref[:, ps]
        x2 = x * x
        ms_even = jnp.sum(jnp.where(odd_lane, 0.0, x2), axis=-1, keepdims=True) * (1.0 / SB_DH)
        ms_odd = jnp.sum(jnp.where(odd_lane, x2, 0.0), axis=-1, keepdims=True) * (1.0 / SB_DH)
        scale = jnp.where(odd_lane, lax.rsqrt(ms_odd + EPS), lax.rsqrt(ms_even + EPS))
        o_ref[0, :, ps] = x * scale * g_ref[:, ps]


def _sb(q, k, v, norm_g):
    b, l, d = q.shape
    nq = (l - LEAD) // BLK
    full = pl.BlockSpec((1, l, d), lambda bi, qi: (bi, 0, 0), pipeline_mode=pl.Buffered(1))
    return pl.pallas_call(
        _sb_kernel,
        grid=(b, nq),
        in_specs=[pl.BlockSpec((1, BLK, d), lambda bi, qi: (bi, qi + LEAD // BLK, 0)),
                  full, full, pl.BlockSpec((1, d), lambda bi, qi: (0, 0))],
        out_specs=pl.BlockSpec((1, BLK, d), lambda bi, qi: (bi, qi, 0)),
        out_shape=jax.ShapeDtypeStruct((b, l - LEAD, d), F32),
        scratch_shapes=[pltpu.VMEM((SB_HEADS, BLK, BLK), F32), pltpu.VMEM((BLK, d), F32)],
        compiler_params=pltpu.CompilerParams(dimension_semantics=("parallel", "arbitrary"),
                                             vmem_limit_bytes=VMEM_LIMIT),
        name="stickbreak",
    )(q, k, v, norm_g)


def _outproj_kernel(x_ref, ym_ref, ys_ref, wm_ref, ws_ref, g_ref, h_ref, n_ref):
    h = x_ref[0] + jnp.dot(ym_ref[0].astype(BF16), wm_ref[...], preferred_element_type=F32) \
        + jnp.dot(ys_ref[0].astype(BF16), ws_ref[...], preferred_element_type=F32)
    h_ref[0] = h
    n_ref[0] = (h * _rms_scale(h) * g_ref[...]).astype(BF16)


def _outproj(x, ym, ys, w_m, w_s, g, tm=512):
    b, s, d = x.shape
    blk = lambda bi, i: (bi, i, 0)
    c2 = lambda bi, i: (0, 0)
    return pl.pallas_call(
        _outproj_kernel,
        grid=(b, s // tm),
        in_specs=[pl.BlockSpec((1, tm, d), blk), pl.BlockSpec((1, tm, D_MLSTM), blk),
                  pl.BlockSpec((1, tm, D_SB), blk),
                  pl.BlockSpec(w_m.shape, c2), pl.BlockSpec(w_s.shape, c2), pl.BlockSpec((1, d), c2)],
        out_specs=[pl.BlockSpec((1, tm, d), blk), pl.BlockSpec((1, tm, d), blk)],
        out_shape=[jax.ShapeDtypeStruct((b, s, d), F32), jax.ShapeDtypeStruct((b, s, d), BF16)],
        compiler_params=pltpu.CompilerParams(dimension_semantics=("parallel", "parallel"),
                                             vmem_limit_bytes=VMEM_LIMIT),
        name="outproj",
    )(x, ym, ys, w_m, w_s, g)


LOG2E = 1.4426950408889634


def _top_values(s, k):
    vals = []
    for _ in range(k):
        m = jnp.max(s, axis=0, keepdims=True)
        vals.append(m)
        s = jnp.where(s == m, NEG_INF, s)
    return vals


def _pair_candidates(top1, top2):
    cand = []
    for r1 in range(PEER_TOPK):
        cand.append(top1[r1] + jnp.concatenate(top2[:PEER_TOPK // (r1 + 1)], axis=0))
    return jnp.concatenate(cand, axis=0)


def _route_kernel(n_ref, wq_ref, keys_ref, s1_ref, s2_ref, tau_ref):
    qt = lax.dot_general(wq_ref[...], n_ref[...], (((1,), (1,)), ((), ())), preferred_element_type=F32)
    for h in range(PEER_HEADS):
        shifted = []
        tops = []
        for p in range(2):
            r0 = (h * 2 + p) * PEER_HALF
            sc = jnp.dot(keys_ref[p], qt[r0:r0 + PEER_HALF, :].astype(BF16), preferred_element_type=F32)
            top = _top_values(sc, PEER_TOPK)
            shifted.append((sc - top[0]) * LOG2E)
            tops.append([(t - top[0]) * LOG2E for t in top])
        cand = _pair_candidates(tops[0], tops[1])
        tau = _top_values(cand, PEER_TOPK)[-1]
        z = jnp.sum(jnp.where(cand >= tau, jnp.exp2(cand), 0.0), axis=0, keepdims=True)
        log2z = jnp.log(z) * LOG2E
        cand = _pair_candidates(tops[0], [t - log2z for t in tops[1]])
        s1_ref[h] = shifted[0]
        s2_ref[h] = shifted[1] - log2z
        tau_ref[h] = _top_values(cand, PEER_TOPK)[-1]


def _route(n2d, wq_t, keys16, tb=256):
    m = n2d.shape[0]
    big = pl.BlockSpec((PEER_HEADS, N_KEYS, tb), lambda i: (0, 0, i))
    small = pl.BlockSpec((PEER_HEADS, 1, tb), lambda i: (0, 0, i))
    return pl.pallas_call(
        _route_kernel,
        grid=(m // tb,),
        in_specs=[pl.BlockSpec((tb, D_MODEL), lambda i: (i, 0)),
                  pl.BlockSpec(wq_t.shape, lambda i: (0, 0)),
                  pl.BlockSpec(keys16.shape, lambda i: (0, 0, 0))],
        out_specs=[big, big, small],
        out_shape=[jax.ShapeDtypeStruct((PEER_HEADS, N_KEYS, m), F32)] * 2
        + [jax.ShapeDtypeStruct((PEER_HEADS, 1, m), F32)],
        compiler_params=pltpu.CompilerParams(dimension_semantics=("parallel",),
                                             vmem_limit_bytes=VMEM_LIMIT),
        name="peer_route",
    )(n2d, wq_t, keys16)


def _expert_kernel(n_ref, u_ref, v_ref, s1_ref, s2_ref, tau_ref, h_ref, g_ref, o_ref, acc_ref, act_ref,
                   wa_ref):
    ei = pl.program_id(1)
    eb, tb = act_ref.shape

    @pl.when(ei == 0)
    def _():
        acc_ref[...] = jnp.zeros_like(acc_ref)

    chunk = 2 * N_KEYS
    n_chunks = eb // chunk

    tok_w = 256
    out_w = 256

    def activations(c, p):
        rows = slice(c * chunk, (c + 1) * chunk)
        cols = slice(p * tok_w, (p + 1) * tok_w)
        a = lax.dot_general(u_ref[rows, :], n_ref[cols, :], (((1,), (1,)), ((), ())),
                            preferred_element_type=F32)
        act_ref[rows, cols] = 0.5 * a * (1.0 + lax.erf(a * (2.0 ** -0.5)))

    def gates(r, t):
        rs = slice(r * N_KEYS, (r + 1) * N_KEYS)
        ts = slice(t * 128, (t + 1) * 128)
        gate = None
        for h in range(PEER_HEADS):
            pair = s1_ref[h, r:r + 1, ts] + s2_ref[h, :, ts]
            wgt = jnp.where(pair >= tau_ref[h, :, ts], jnp.exp2(pair), 0.0)
            gate = wgt if gate is None else gate + wgt
        wa_ref[rs, ts] = (gate * act_ref[rs, ts]).astype(BF16)

    def outputs(c, p):
        rows = slice(c * chunk, (c + 1) * chunk)
        cols = slice(p * out_w, (p + 1) * out_w)
        acc_ref[:, cols] += lax.dot_general(wa_ref[rows, :], v_ref[rows, cols], (((0,), (0,)), ((), ())),
                                            preferred_element_type=F32)

    tiles = [(r, t) for r in range(chunk // N_KEYS) for t in range(tb // 128)]
    for stage in range(-1, n_chunks + 1):
        mxu_work = []
        if 0 <= stage + 1 < n_chunks:
            mxu_work += [functools.partial(activations, stage + 1, p) for p in range(tb // tok_w)]
        if 0 <= stage - 1 < n_chunks:
            mxu_work += [functools.partial(outputs, stage - 1, p) for p in range(D_MODEL // out_w)]
        vpu_work = []
        if 0 <= stage < n_chunks:
            vpu_work = [functools.partial(gates, stage * (chunk // N_KEYS) + r, t) for r, t in tiles]
        while mxu_work or vpu_work:
            if vpu_work:
                vpu_work.pop(0)()
            if mxu_work:
                mxu_work.pop(0)()

    @pl.when(ei == pl.num_programs(1) - 1)
    def _():
        h = h_ref[...] + acc_ref[...]
        o_ref[...] = h * _rms_scale(h) * g_ref[...]


def _experts(n2d, u16, v16, s1, s2, tau, h2d, g, tb=512, eb=1024):
    m = n2d.shape[0]
    ne = u16.shape[0]
    rows = eb // N_KEYS
    tok = lambda ti, ei: (ti, 0)
    exp = lambda ti, ei: (ei, 0)
    return pl.pallas_call(
        _expert_kernel,
        grid=(m // tb, ne // eb),
        in_specs=[pl.BlockSpec((tb, D_MODEL), tok),
                  pl.BlockSpec((eb, D_MODEL), exp),
                  pl.BlockSpec((eb, D_MODEL), exp),
                  pl.BlockSpec((PEER_HEADS, rows, tb), lambda ti, ei: (0, ei, ti)),
                  pl.BlockSpec((PEER_HEADS, N_KEYS, tb), lambda ti, ei: (0, 0, ti)),
                  pl.BlockSpec((PEER_HEADS, 1, tb), lambda ti, ei: (0, 0, ti)),
                  pl.BlockSpec((tb, D_MODEL), tok),
                  pl.BlockSpec((1, D_MODEL), lambda ti, ei: (0, 0))],
        out_specs=pl.BlockSpec((tb, D_MODEL), tok),
        out_shape=jax.ShapeDtypeStruct((m, D_MODEL), F32),
        scratch_shapes=[pltpu.VMEM((tb, D_MODEL), F32), pltpu.VMEM((eb, tb), F32),
                        pltpu.VMEM((eb, tb), BF16)],
        compiler_params=pltpu.CompilerParams(dimension_semantics=("parallel", "arbitrary"),
                                             vmem_limit_bytes=VMEM_LIMIT),
        name="peer_experts",
    )(n2d, u16, v16, s1, s2, tau, h2d, g)


def kernel(x, meta, norm_mix_g, w_in, conv_w, conv_b, w_qm, w_km, b_i, b_f, mlstm_skip, mlstm_gn_g,
           sb_norm_g, w_out, norm_ffn_g, w_query, sub_keys, expert_u, expert_v, norm_final_g):
    b, s, d = x.shape
    depth = w_in.shape[0]
    assert depth == 1 and d == D_MODEL and s % 512 == 0
    l = s + LEAD
    lead = jnp.concatenate([jnp.zeros((PAD, d), x.dtype), meta.astype(x.dtype)], axis=0)
    h = jnp.concatenate([jnp.broadcast_to(lead[None], (b, LEAD, d)), x], axis=1)

    w = w_in[0]
    o_u, o_v, o_o = 0, D_MLSTM, 2 * D_MLSTM
    o_i = 3 * D_MLSTM
    o_f = o_i + MLSTM_HEADS
    o_q = o_f + MLSTM_HEADS
    w_main = jnp.concatenate([w[:, o_u:o_i], w[:, o_q:]], axis=1).astype(BF16)
    w_gate = jnp.pad(w[:, o_i:o_q], ((0, 0), (0, GATE_LANES - 2 * MLSTM_HEADS))).astype(BF16)
    gate_bias = jnp.pad(jnp.concatenate([b_i[0], b_f[0]]), (0, GATE_LANES - 2 * MLSTM_HEADS))[None, :]

    u, vm, o_pre, gates, qs, ks, vs = _inproj(h.reshape(b * l, d), norm_mix_g[0][None, :], w_main, w_gate)

    r3 = lambda t: t.reshape(b, l, -1)
    ym = _mlstm(r3(u), r3(vm), r3(o_pre), r3(gates), conv_w[0], conv_b[0][None, :],
                w_qm[0].astype(BF16), w_km[0].astype(BF16), gate_bias,
                mlstm_skip[0][None, :], mlstm_gn_g[0][None, :])

    ys = _sb(r3(qs), r3(ks), r3(vs), sb_norm_g[0][None, :])

    wo = w_out[0].astype(BF16)
    h2, n2 = _outproj(x, ym, ys, wo[:D_MLSTM], wo[D_MLSTM:], norm_ffn_g[0][None, :])

    n2d = n2.reshape(b * s, d)
    s1, s2, tau = _route(n2d, w_query[0].T.astype(BF16), sub_keys[0].astype(BF16))
    out = _experts(n2d, expert_u[0].astype(BF16), expert_v[0].astype(BF16), s1, s2, tau,
                   h2.reshape(b * s, d), norm_final_g[None, :])
    return out.reshape(b, s, d)
```

```python
import functools

import jax
import jax.numpy as jnp
from jax import lax
from jax.experimental import pallas as pl
from jax.experimental.pallas import tpu as pltpu

F32 = jnp.float32
BF16 = jnp.bfloat16

D_MODEL = 1024
D_MLSTM = 512
D_SB = 512
MLSTM_HEADS = 4
MLSTM_DH = 128
SB_HEADS = 8
SB_DH = 64
CONV_W = 4
N_META = 16
LEAD = 128
PAD = LEAD - N_META
EPS = 1e-6

PEER_HEADS = 8
N_KEYS = 128
PEER_TOPK = 16
PEER_HALF = 128

SUBLANES = 8
GATE_LANES = 128
BLK = 128
NEG_INF = float("-inf")
SB_DEAD_LOG = -105.0

VMEM_LIMIT = 56 * 1024 * 1024


def _softplus(z):
    return jnp.maximum(z, 0.0) + jnp.log1p(jnp.exp(-jnp.abs(z)))


def _rms_scale(x):
    return lax.rsqrt(jnp.mean(x * x, axis=-1, keepdims=True) + EPS)


def _inproj_kernel(h_ref, g_ref, wm_ref, wg_ref, u_ref, vm_ref, o_ref, gate_ref, q_ref, k_ref, v_ref):
    x = h_ref[...]
    n = (x * _rms_scale(x) * g_ref[...]).astype(BF16)
    y = jnp.dot(n, wm_ref[...], preferred_element_type=F32)
    u_ref[...] = y[:, 0 * 512:1 * 512]
    vm_ref[...] = y[:, 1 * 512:2 * 512]
    o_ref[...] = y[:, 2 * 512:3 * 512]
    q_ref[...] = y[:, 3 * 512:4 * 512].astype(BF16)
    k_ref[...] = y[:, 4 * 512:5 * 512].astype(BF16)
    v_ref[...] = y[:, 5 * 512:6 * 512].astype(BF16)
    gate_ref[...] = jnp.dot(n, wg_ref[...], preferred_element_type=F32)


def _inproj(h2d, g, w_main, w_gate, tm=256):
    m = h2d.shape[0]
    row = lambda i: (i, 0)
    const = lambda i: (0, 0)
    o512 = pl.BlockSpec((tm, 512), row)
    return pl.pallas_call(
        _inproj_kernel,
        grid=(m // tm,),
        in_specs=[pl.BlockSpec((tm, D_MODEL), row), pl.BlockSpec((1, D_MODEL), const),
                  pl.BlockSpec(w_main.shape, const), pl.BlockSpec(w_gate.shape, const)],
        out_specs=[o512, o512, o512, pl.BlockSpec((tm, GATE_LANES), row), o512, o512, o512],
        out_shape=[jax.ShapeDtypeStruct((m, 512), F32)] * 3
        + [jax.ShapeDtypeStruct((m, GATE_LANES), F32)]
        + [jax.ShapeDtypeStruct((m, 512), BF16)] * 3,
        compiler_params=pltpu.CompilerParams(dimension_semantics=("parallel",),
                                             vmem_limit_bytes=VMEM_LIMIT),
        name="inproj",
    )(h2d, g, w_main, w_gate)


def _mlstm_kernel(u_ref, vm_ref, o_ref, gate_ref, cw_ref, cb_ref, wq_ref, wk_ref, gb_ref, skip_ref,
                  gn_ref, y_ref, c_state, n_state, m_state, u_tail):
    ci = pl.program_id(1)

    @pl.when(ci == 0)
    def _():
        c_state[...] = jnp.zeros_like(c_state)
        n_state[...] = jnp.zeros_like(n_state)
        m_state[...] = jnp.zeros_like(m_state)
        u_tail[...] = jnp.zeros_like(u_tail)

    tpos = ci * BLK + lax.broadcasted_iota(jnp.int32, (BLK, 1), 0)
    valid = tpos >= PAD

    u = jnp.where(valid, u_ref[0], 0.0)
    ext = jnp.concatenate([u_tail[...], u], axis=0)
    u_tail[...] = u[BLK - 8:, :]
    cw = cw_ref[...]
    c = cb_ref[...] + cw[CONV_W - 1:CONV_W, :] * u
    for j in range(CONV_W - 1):
        off = 8 - (CONV_W - 1) + j
        c = c + cw[j:j + 1, :] * ext[off:off + BLK, :]
    c = c * jax.nn.sigmoid(c)
    c16 = c.astype(BF16)

    gpre = gate_ref[0] + gb_ref[...]
    logi = jnp.where(valid, gpre, NEG_INF)
    logf = jnp.where(valid, -_softplus(-gpre), 0.0)
    r = lax.broadcasted_iota(jnp.int32, (BLK, BLK), 0)
    s = lax.broadcasted_iota(jnp.int32, (BLK, BLK), 1)
    tril = s <= r
    bcum = jnp.dot(tril.astype(F32), logf, preferred_element_type=F32,
                   precision=lax.Precision.HIGHEST)
    logi_t = logi.T
    bcum_t = bcum.T

    vm = vm_ref[0]
    outs = []
    for h in range(MLSTM_HEADS):
        hs = slice(h * MLSTM_DH, (h + 1) * MLSTM_DH)
        ch = c16[:, hs]
        qh = jnp.dot(ch, wq_ref[h], preferred_element_type=F32)
        kh = jnp.dot(ch, wk_ref[h], preferred_element_type=F32) * (MLSTM_DH ** -0.5)
        vh = vm[:, hs].astype(BF16)
        q16 = qh.astype(BF16)
        k16 = kh.astype(BF16)

        b_col = bcum[:, MLSTM_HEADS + h:MLSTM_HEADS + h + 1]
        li_col = logi[:, h:h + 1]
        b_row = bcum_t[MLSTM_HEADS + h:MLSTM_HEADS + h + 1, :]
        li_row = logi_t[h:h + 1, :]
        m_prev = m_state[h]
        c_prev = c_state[h]
        n_prev = n_state[h]

        dmat = jnp.where(tril, b_col - b_row + li_row, NEG_INF)
        inter = b_col + m_prev
        m_t = jnp.maximum(inter, jnp.max(dmat, axis=-1, keepdims=True))
        w_intra = jnp.exp(dmat - m_t)
        w_inter = jnp.exp(inter - m_t)
        sc = lax.dot_general(q16, k16, (((1,), (1,)), ((), ())), preferred_element_type=F32) * w_intra
        num = w_inter * jnp.dot(q16, c_prev.astype(BF16), preferred_element_type=F32) \
            + jnp.dot(sc.astype(BF16), vh, preferred_element_type=F32)
        den = w_inter * jnp.sum(qh * n_prev, axis=-1, keepdims=True) + jnp.sum(sc, axis=-1, keepdims=True)
        outs.append(num / jnp.maximum(jnp.abs(den), jnp.exp(-m_t)))

        b_last = b_col[BLK - 1:BLK, :]
        g_row = b_last - b_row + li_row
        g_col = b_last - b_col + li_col
        m_new = jnp.maximum(b_last + m_prev, jnp.max(g_row, axis=-1, keepdims=True))
        decay = jnp.exp(b_last + m_prev - m_new)
        kw = kh * jnp.exp(g_col - m_new)
        c_state[h] = decay * c_prev + jnp.dot(kw.T.astype(BF16), vh, preferred_element_type=F32)
        n_state[h] = decay * n_prev + jnp.sum(kw, axis=0, keepdims=True)
        m_state[h] = m_new

    o_gate = jax.nn.sigmoid(o_ref[0])
    normed = []
    for h in range(MLSTM_HEADS):
        hs = slice(h * MLSTM_DH, (h + 1) * MLSTM_DH)
        hh = outs[h] * o_gate[:, hs]
        mu = jnp.mean(hh, axis=-1, keepdims=True)
        var = jnp.mean(jnp.square(hh - mu), axis=-1, keepdims=True)
        normed.append((hh - mu) * lax.rsqrt(var + EPS))
    hn = jnp.concatenate(normed, axis=-1)
    y_ref[0] = hn * gn_ref[...] + skip_ref[...] * c


def _mlstm(u, vm, o_pre, gates, conv_w, conv_b, wq, wk, gate_bias, skip, gn_g):
    b, l, _ = u.shape
    blk = lambda bi, ci: (bi, ci, 0)
    c2 = lambda bi, ci: (0, 0)
    c3 = lambda bi, ci: (0, 0, 0)
    t512 = pl.BlockSpec((1, BLK, D_MLSTM), blk)
    return pl.pallas_call(
        _mlstm_kernel,
        grid=(b, l // BLK),
        in_specs=[t512, t512, t512, pl.BlockSpec((1, BLK, GATE_LANES), blk),
                  pl.BlockSpec((CONV_W, D_MLSTM), c2), pl.BlockSpec((1, D_MLSTM), c2),
                  pl.BlockSpec(wq.shape, c3), pl.BlockSpec(wk.shape, c3),
                  pl.BlockSpec((1, GATE_LANES), c2), pl.BlockSpec((1, D_MLSTM), c2),
                  pl.BlockSpec((1, D_MLSTM), c2)],
        out_specs=pl.BlockSpec((1, BLK, D_MLSTM), lambda bi, ci: (bi, jnp.maximum(ci - LEAD // BLK, 0), 0)),
        out_shape=jax.ShapeDtypeStruct((b, l - LEAD, D_MLSTM), F32),
        scratch_shapes=[pltpu.VMEM((MLSTM_HEADS, MLSTM_DH, MLSTM_DH), F32),
                        pltpu.VMEM((MLSTM_HEADS, 1, MLSTM_DH), F32),
                        pltpu.VMEM((MLSTM_HEADS, 1, 1), F32),
                        pltpu.VMEM((8, D_MLSTM), F32)],
        compiler_params=pltpu.CompilerParams(dimension_semantics=("parallel", "arbitrary"),
                                             vmem_limit_bytes=VMEM_LIMIT),
        name="mlstm",
    )(u, vm, o_pre, gates, conv_w, conv_b, wq, wk, gate_bias, skip, gn_g)


PAIR = 2 * SB_DH
N_PAIRS = SB_HEADS // 2


def _sb_kernel(q_ref, k_ref, v_ref, g_ref, o_ref, run_ref, acc_ref):
    qblk = pl.program_id(1) + LEAD // BLK
    row = lax.broadcasted_iota(jnp.int32, (BLK, BLK), 0)
    col = lax.broadcasted_iota(jnp.int32, (BLK, BLK), 1)
    odd_lane = col >= SB_DH
    jj = lax.broadcasted_iota(jnp.int32, (BLK, 2 * BLK), 0)
    ss = lax.broadcasted_iota(jnp.int32, (BLK, 2 * BLK), 1)
    tri2 = jnp.logical_or(jj > ss, ss >= BLK).astype(BF16)

    qf = q_ref[0].astype(F32) * (SB_DH ** -0.5)
    qm = []
    for h in range(SB_HEADS):
        qp = qf[:, (h // 2) * PAIR:(h // 2 + 1) * PAIR]
        qm.append(jnp.where(odd_lane if h % 2 else jnp.logical_not(odd_lane), qp, 0.0).astype(BF16))

    q_pairs = [jnp.concatenate([qm[2 * p], qm[2 * p + 1]], axis=0) for p in range(N_PAIRS)]

    def key_block(start, mask, first):
        kblk = k_ref[0, pl.ds(start, BLK), :]
        vblk = v_ref[0, pl.ds(start, BLK), :]
        z = jnp.concatenate(
            [lax.dot_general(q_pairs[p], kblk[:, p * PAIR:(p + 1) * PAIR], (((1,), (1,)), ((), ())),
                             preferred_element_type=F32) for p in range(N_PAIRS)], axis=0)
        z = z.reshape(SB_HEADS, BLK, BLK)
        lneg = jnp.where(mask[None], -_softplus(z), 0.0)
        flat = lneg.reshape(SB_HEADS * BLK, BLK)
        hi = flat.astype(BF16)
        lo = (flat - hi.astype(F32)).astype(BF16)
        rt = jnp.dot(hi, tri2, preferred_element_type=F32) + jnp.dot(lo, tri2, preferred_element_type=F32)
        rex = rt[:, :BLK].reshape(SB_HEADS, BLK, BLK)
        tot = rt[:, BLK:].reshape(SB_HEADS, BLK, BLK)
        e = z + lneg + rex
        if not first:
            e = e + run_ref[...]
        a = jnp.where(mask[None], jnp.exp(e), 0.0).astype(BF16)
        run_ref[...] = tot if first else run_ref[...] + tot
        for p in range(N_PAIRS):
            vp = vblk[:, p * PAIR:(p + 1) * PAIR]
            zero = jnp.zeros_like(vp)
            v_heads = jnp.concatenate([jnp.where(odd_lane, zero, vp), jnp.where(odd_lane, vp, zero)], axis=0)
            pv = jnp.dot(jnp.concatenate([a[2 * p], a[2 * p + 1]], axis=1), v_heads,
                         preferred_element_type=F32)
            ps = slice(p * PAIR, (p + 1) * PAIR)
            acc_ref[:, ps] = pv if first else acc_ref[:, ps] + pv

    def alive():
        m = run_ref[0]
        for h in range(1, SB_HEADS):
            m = jnp.maximum(m, run_ref[h])
        return (jnp.max(m) > SB_DEAD_LOG).astype(jnp.int32)

    key_block(pl.multiple_of(qblk * BLK, BLK), col < row, True)

    def cond(carry):
        kb, go = carry
        return jnp.logical_and(kb >= 0, go > 0)

    def body(carry):
        kb, _ = carry
        start = pl.multiple_of(kb * BLK, BLK)
        key_block(start, start + col >= PAD, False)
        return kb - 1, alive()

    lax.while_loop(cond, body, (qblk - 1, alive()))

    for p in range(N_PAIRS):
        ps = slice(p * PAIR, (p + 1) * PAIR)
        x = acc_ref[:, ps]
        x2 = x * x
        ms_even = jnp.sum(jnp.where(odd_lane, 0.0, x2), axis=-1, keepdims=True) * (1.0 / SB_DH)
        ms_odd = jnp.sum(jnp.where(odd_lane, x2, 0.0), axis=-1, keepdims=True) * (1.0 / SB_DH)
        scale = jnp.where(odd_lane, lax.rsqrt(ms_odd + EPS), lax.rsqrt(ms_even + EPS))
        o_ref[0, :, ps] = x * scale * g_ref[:, ps]


def _sb(q, k, v, norm_g):
    b, l, d = q.shape
    nq = (l - LEAD) // BLK
    full = pl.BlockSpec((1, l, d), lambda bi, qi: (bi, 0, 0), pipeline_mode=pl.Buffered(1))
    return pl.pallas_call(
        _sb_kernel,
        grid=(b, nq),
        in_specs=[pl.BlockSpec((1, BLK, d), lambda bi, qi: (bi, qi + LEAD // BLK, 0)),
                  full, full, pl.BlockSpec((1, d), lambda bi, qi: (0, 0))],
        out_specs=pl.BlockSpec((1, BLK, d), lambda bi, qi: (bi, qi, 0)),
        out_shape=jax.ShapeDtypeStruct((b, l - LEAD, d), F32),
        scratch_shapes=[pltpu.VMEM((SB_HEADS, BLK, BLK), F32), pltpu.VMEM((BLK, d), F32)],
        compiler_params=pltpu.CompilerParams(dimension_semantics=("parallel", "arbitrary"),
                                             vmem_limit_bytes=VMEM_LIMIT),
        name="stickbreak",
    )(q, k, v, norm_g)


def _outproj_kernel(x_ref, ym_ref, ys_ref, wm_ref, ws_ref, g_ref, h_ref, n_ref):
    h = x_ref[0] + jnp.dot(ym_ref[0].astype(BF16), wm_ref[...], preferred_element_type=F32) \
        + jnp.dot(ys_ref[0].astype(BF16), ws_ref[...], preferred_element_type=F32)
    h_ref[0] = h
    n_ref[0] = (h * _rms_scale(h) * g_ref[...]).astype(BF16)


def _outproj(x, ym, ys, w_m, w_s, g, tm=512):
    b, s, d = x.shape
    blk = lambda bi, i: (bi, i, 0)
    c2 = lambda bi, i: (0, 0)
    return pl.pallas_call(
        _outproj_kernel,
        grid=(b, s // tm),
        in_specs=[pl.BlockSpec((1, tm, d), blk), pl.BlockSpec((1, tm, D_MLSTM), blk),
                  pl.BlockSpec((1, tm, D_SB), blk),
                  pl.BlockSpec(w_m.shape, c2), pl.BlockSpec(w_s.shape, c2), pl.BlockSpec((1, d), c2)],
        out_specs=[pl.BlockSpec((1, tm, d), blk), pl.BlockSpec((1, tm, d), blk)],
        out_shape=[jax.ShapeDtypeStruct((b, s, d), F32), jax.ShapeDtypeStruct((b, s, d), BF16)],
        compiler_params=pltpu.CompilerParams(dimension_semantics=("parallel", "parallel"),
                                             vmem_limit_bytes=VMEM_LIMIT),
        name="outproj",
    )(x, ym, ys, w_m, w_s, g)


LOG2E = 1.4426950408889634


def _top_values(s, k):
    vals = []
    for _ in range(k):
        m = jnp.max(s, axis=0, keepdims=True)
        vals.append(m)
        s = jnp.where(s == m, NEG_INF, s)
    return vals


def _exchange(a, i, l, descending):
    hi = jnp.maximum(a[i], a[l])
    lo = jnp.minimum(a[i], a[l])
    a[i], a[l] = (hi, lo) if descending else (lo, hi)


def _bitonic_merge(a):
    j = len(a) // 2
    while j >= 1:
        for i in range(len(a)):
            if i ^ j > i:
                _exchange(a, i, i ^ j, True)
        j //= 2


def _sorted_top16(sc):
    n = PEER_TOPK
    a = [sc[SUBLANES * g:SUBLANES * (g + 1), :] for g in range(n)]
    k = 2
    while k <= n:
        j = k // 2
        while j >= 1:
            for i in range(n):
                if i ^ j > i:
                    _exchange(a, i, i ^ j, (i & k) == 0)
            j //= 2
        k *= 2
    shift = SUBLANES // 2
    while shift >= 1:
        b = [pltpu.roll(x, shift, axis=0) for x in a]
        a = [jnp.maximum(a[i], b[n - 1 - i]) for i in range(n)]
        _bitonic_merge(a)
        shift //= 2
    return [x[0:1, :] for x in a]


def _pair_candidates(top1, top2):
    cand = []
    for r1 in range(PEER_TOPK):
        cand.append(top1[r1] + jnp.concatenate(top2[:PEER_TOPK // (r1 + 1)], axis=0))
    return jnp.concatenate(cand, axis=0)


def _route_kernel(n_ref, wq_ref, keys_ref, s1_ref, s2_ref, tau_ref):
    qt = lax.dot_general(wq_ref[...], n_ref[...], (((1,), (1,)), ((), ())), preferred_element_type=F32)
    for h in range(PEER_HEADS):
        shifted = []
        tops = []
        for p in range(2):
            r0 = (h * 2 + p) * PEER_HALF
            sc = jnp.dot(keys_ref[p], qt[r0:r0 + PEER_HALF, :].astype(BF16), preferred_element_type=F32)
            top = _sorted_top16(sc)
            shifted.append((sc - top[0]) * LOG2E)
            tops.append([(t - top[0]) * LOG2E for t in top])
        cand = _pair_candidates(tops[0], tops[1])
        tau = _top_values(cand, PEER_TOPK)[-1]
        z = jnp.sum(jnp.where(cand >= tau, jnp.exp2(cand), 0.0), axis=0, keepdims=True)
        log2z = jnp.log(z) * LOG2E
        cand = _pair_candidates(tops[0], [t - log2z for t in tops[1]])
        s1_ref[h] = shifted[0]
        s2_ref[h] = shifted[1] - log2z
        tau_ref[h] = _top_values(cand, PEER_TOPK)[-1]


def _route(n2d, wq_t, keys16, tb=256):
    m = n2d.shape[0]
    big = pl.BlockSpec((PEER_HEADS, N_KEYS, tb), lambda i: (0, 0, i))
    small = pl.BlockSpec((PEER_HEADS, 1, tb), lambda i: (0, 0, i))
    return pl.pallas_call(
        _route_kernel,
        grid=(m // tb,),
        in_specs=[pl.BlockSpec((tb, D_MODEL), lambda i: (i, 0)),
                  pl.BlockSpec(wq_t.shape, lambda i: (0, 0)),
                  pl.BlockSpec(keys16.shape, lambda i: (0, 0, 0))],
        out_specs=[big, big, small],
        out_shape=[jax.ShapeDtypeStruct((PEER_HEADS, N_KEYS, m), F32)] * 2
        + [jax.ShapeDtypeStruct((PEER_HEADS, 1, m), F32)],
        compiler_params=pltpu.CompilerParams(dimension_semantics=("parallel",),
                                             vmem_limit_bytes=VMEM_LIMIT),
        name="peer_route",
    )(n2d, wq_t, keys16)


def _expert_kernel(n_ref, u_ref, v_ref, s1_ref, s2_ref, tau_ref, h_ref, g_ref, o_ref, acc_ref, act_ref,
                   wa_ref):
    ei = pl.program_id(1)
    eb, tb = act_ref.shape

    @pl.when(ei == 0)
    def _():
        acc_ref[...] = jnp.zeros_like(acc_ref)

    chunk = 2 * N_KEYS
    n_chunks = eb // chunk

    tok_w = 256
    out_w = 256

    def activations(c, p):
        rows = slice(c * chunk, (c + 1) * chunk)
        cols = slice(p * tok_w, (p + 1) * tok_w)
        a = lax.dot_general(u_ref[rows, :], n_ref[cols, :], (((1,), (1,)), ((), ())),
                            preferred_element_type=F32)
        act_ref[rows, cols] = 0.5 * a * (1.0 + lax.erf(a * (2.0 ** -0.5)))

    def gates(r, t):
        rs = slice(r * N_KEYS, (r + 1) * N_KEYS)
        ts = slice(t * 128, (t + 1) * 128)
        gate = None
        for h in range(PEER_HEADS):
            pair = s1_ref[h, r:r + 1, ts] + s2_ref[h, :, ts]
            wgt = jnp.where(pair >= tau_ref[h, :, ts], jnp.exp2(pair), 0.0)
            gate = wgt if gate is None else gate + wgt
        wa_ref[rs, ts] = (gate * act_ref[rs, ts]).astype(BF16)

    def outputs(c, p):
        rows = slice(c * chunk, (c + 1) * chunk)
        cols = slice(p * out_w, (p + 1) * out_w)
        acc_ref[:, cols] += lax.dot_general(wa_ref[rows, :], v_ref[rows, cols], (((0,), (0,)), ((), ())),
                                            preferred_element_type=F32)

    tiles = [(r, t) for r in range(chunk // N_KEYS) for t in range(tb // 128)]
    for stage in range(-1, n_chunks + 1):
        mxu_work = []
        if 0 <= stage + 1 < n_chunks:
            mxu_work += [functools.partial(activations, stage + 1, p) for p in range(tb // tok_w)]
        if 0 <= stage - 1 < n_chunks:
            mxu_work += [functools.partial(outputs, stage - 1, p) for p in range(D_MODEL // out_w)]
        vpu_work = []
        if 0 <= stage < n_chunks:
            vpu_work = [functools.partial(gates, stage * (chunk // N_KEYS) + r, t) for r, t in tiles]
        while mxu_work or vpu_work:
            if vpu_work:
                vpu_work.pop(0)()
            if mxu_work:
                mxu_work.pop(0)()

    @pl.when(ei == pl.num_programs(1) - 1)
    def _():
        h = h_ref[...] + acc_ref[...]
        o_ref[...] = h * _rms_scale(h) * g_ref[...]


def _experts(n2d, u16, v16, s1, s2, tau, h2d, g, tb=512, eb=1024):
    m = n2d.shape[0]
    ne = u16.shape[0]
    rows = eb // N_KEYS
    tok = lambda ti, ei: (ti, 0)
    exp = lambda ti, ei: (ei, 0)
    return pl.pallas_call(
        _expert_kernel,
        grid=(m // tb, ne // eb),
        in_specs=[pl.BlockSpec((tb, D_MODEL), tok),
                  pl.BlockSpec((eb, D_MODEL), exp),
                  pl.BlockSpec((eb, D_MODEL), exp),
                  pl.BlockSpec((PEER_HEADS, rows, tb), lambda ti, ei: (0, ei, ti)),
                  pl.BlockSpec((PEER_HEADS, N_KEYS, tb), lambda ti, ei: (0, 0, ti)),
                  pl.BlockSpec((PEER_HEADS, 1, tb), lambda ti, ei: (0, 0, ti)),
                  pl.BlockSpec((tb, D_MODEL), tok),
                  pl.BlockSpec((1, D_MODEL), lambda ti, ei: (0, 0))],
        out_specs=pl.BlockSpec((tb, D_MODEL), tok),
        out_shape=jax.ShapeDtypeStruct((m, D_MODEL), F32),
        scratch_shapes=[pltpu.VMEM((tb, D_MODEL), F32), pltpu.VMEM((eb, tb), F32),
                        pltpu.VMEM((eb, tb), BF16)],
        compiler_params=pltpu.CompilerParams(dimension_semantics=("parallel", "arbitrary"),
                                             vmem_limit_bytes=VMEM_LIMIT),
        name="peer_experts",
    )(n2d, u16, v16, s1, s2, tau, h2d, g)


def kernel(x, meta, norm_mix_g, w_in, conv_w, conv_b, w_qm, w_km, b_i, b_f, mlstm_skip, mlstm_gn_g,
           sb_norm_g, w_out, norm_ffn_g, w_query, sub_keys, expert_u, expert_v, norm_final_g):
    b, s, d = x.shape
    depth = w_in.shape[0]
    assert depth == 1 and d == D_MODEL and s % 512 == 0
    l = s + LEAD
    lead = jnp.concatenate([jnp.zeros((PAD, d), x.dtype), meta.astype(x.dtype)], axis=0)
    h = jnp.concatenate([jnp.broadcast_to(lead[None], (b, LEAD, d)), x], axis=1)

    w = w_in[0]
    o_u, o_v, o_o = 0, D_MLSTM, 2 * D_MLSTM
    o_i = 3 * D_MLSTM
    o_f = o_i + MLSTM_HEADS
    o_q = o_f + MLSTM_HEADS
    w_main = jnp.concatenate([w[:, o_u:o_i], w[:, o_q:]], axis=1).astype(BF16)
    w_gate = jnp.pad(w[:, o_i:o_q], ((0, 0), (0, GATE_LANES - 2 * MLSTM_HEADS))).astype(BF16)
    gate_bias = jnp.pad(jnp.concatenate([b_i[0], b_f[0]]), (0, GATE_LANES - 2 * MLSTM_HEADS))[None, :]

    u, vm, o_pre, gates, qs, ks, vs = _inproj(h.reshape(b * l, d), norm_mix_g[0][None, :], w_main, w_gate)

    r3 = lambda t: t.reshape(b, l, -1)
    ym = _mlstm(r3(u), r3(vm), r3(o_pre), r3(gates), conv_w[0], conv_b[0][None, :],
                w_qm[0].astype(BF16), w_km[0].astype(BF16), gate_bias,
                mlstm_skip[0][None, :], mlstm_gn_g[0][None, :])

    ys = _sb(r3(qs), r3(ks), r3(vs), sb_norm_g[0][None, :])

    wo = w_out[0].astype(BF16)
    h2, n2 = _outproj(x, ym, ys, wo[:D_MLSTM], wo[D_MLSTM:], norm_ffn_g[0][None, :])

    n2d = n2.reshape(b * s, d)
    s1, s2, tau = _route(n2d, w_query[0].T.astype(BF16), sub_keys[0].astype(BF16))
    out = _experts(n2d, expert_u[0].astype(BF16), expert_v[0].astype(BF16), s1, s2, tau,
                   h2.reshape(b * s, d), norm_final_g[None, :])
    return out.reshape(b, s, d)
```

```python
import functools

import jax
import jax.numpy as jnp
from jax import lax
from jax.experimental import pallas as pl
from jax.experimental.pallas import tpu as pltpu

F32 = jnp.float32
BF16 = jnp.bfloat16

D_MODEL = 1024
D_MLSTM = 512
D_SB = 512
MLSTM_HEADS = 4
MLSTM_DH = 128
SB_HEADS = 8
SB_DH = 64
CONV_W = 4
N_META = 16
LEAD = 128
PAD = LEAD - N_META
EPS = 1e-6

PEER_HEADS = 8
N_KEYS = 128
PEER_TOPK = 16
PEER_HALF = 128

SUBLANES = 8
GATE_LANES = 128
BLK = 128
NEG_INF = float("-inf")
SB_DEAD_LOG = -105.0

VMEM_LIMIT = 56 * 1024 * 1024


def _softplus(z):
    return jnp.maximum(z, 0.0) + jnp.log1p(jnp.exp(-jnp.abs(z)))


def _rms_scale(x):
    return lax.rsqrt(jnp.mean(x * x, axis=-1, keepdims=True) + EPS)


def _inproj_kernel(h_ref, g_ref, wm_ref, wg_ref, u_ref, vm_ref, o_ref, gate_ref, q_ref, k_ref, v_ref):
    x = h_ref[...]
    n = (x * _rms_scale(x) * g_ref[...]).astype(BF16)
    y = jnp.dot(n, wm_ref[...], preferred_element_type=F32)
    u_ref[...] = y[:, 0 * 512:1 * 512]
    vm_ref[...] = y[:, 1 * 512:2 * 512]
    o_ref[...] = y[:, 2 * 512:3 * 512]
    q_ref[...] = y[:, 3 * 512:4 * 512].astype(BF16)
    k_ref[...] = y[:, 4 * 512:5 * 512].astype(BF16)
    v_ref[...] = y[:, 5 * 512:6 * 512].astype(BF16)
    gate_ref[...] = jnp.dot(n, wg_ref[...], preferred_element_type=F32)


def _inproj(h2d, g, w_main, w_gate, tm=256):
    m = h2d.shape[0]
    row = lambda i: (i, 0)
    const = lambda i: (0, 0)
    o512 = pl.BlockSpec((tm, 512), row)
    return pl.pallas_call(
        _inproj_kernel,
        grid=(m // tm,),
        in_specs=[pl.BlockSpec((tm, D_MODEL), row), pl.BlockSpec((1, D_MODEL), const),
                  pl.BlockSpec(w_main.shape, const), pl.BlockSpec(w_gate.shape, const)],
        out_specs=[o512, o512, o512, pl.BlockSpec((tm, GATE_LANES), row), o512, o512, o512],
        out_shape=[jax.ShapeDtypeStruct((m, 512), F32)] * 3
        + [jax.ShapeDtypeStruct((m, GATE_LANES), F32)]
        + [jax.ShapeDtypeStruct((m, 512), BF16)] * 3,
        compiler_params=pltpu.CompilerParams(dimension_semantics=("parallel",),
                                             vmem_limit_bytes=VMEM_LIMIT),
        name="inproj",
    )(h2d, g, w_main, w_gate)


def _mlstm_kernel(u_ref, vm_ref, o_ref, gate_ref, cw_ref, cb_ref, wq_ref, wk_ref, gb_ref, skip_ref,
                  gn_ref, y_ref, c_state, n_state, m_state, u_tail):
    ci = pl.program_id(1)

    @pl.when(ci == 0)
    def _():
        c_state[...] = jnp.zeros_like(c_state)
        n_state[...] = jnp.zeros_like(n_state)
        m_state[...] = jnp.zeros_like(m_state)
        u_tail[...] = jnp.zeros_like(u_tail)

    tpos = ci * BLK + lax.broadcasted_iota(jnp.int32, (BLK, 1), 0)
    valid = tpos >= PAD

    u = jnp.where(valid, u_ref[0], 0.0)
    ext = jnp.concatenate([u_tail[...], u], axis=0)
    u_tail[...] = u[BLK - 8:, :]
    cw = cw_ref[...]
    c = cb_ref[...] + cw[CONV_W - 1:CONV_W, :] * u
    for j in range(CONV_W - 1):
        off = 8 - (CONV_W - 1) + j
        c = c + cw[j:j + 1, :] * ext[off:off + BLK, :]
    c = c * jax.nn.sigmoid(c)
    c16 = c.astype(BF16)

    gpre = gate_ref[0] + gb_ref[...]
    logi = jnp.where(valid, gpre, NEG_INF)
    logf = jnp.where(valid, -_softplus(-gpre), 0.0)
    r = lax.broadcasted_iota(jnp.int32, (BLK, BLK), 0)
    s = lax.broadcasted_iota(jnp.int32, (BLK, BLK), 1)
    tril = s <= r
    bcum = jnp.dot(tril.astype(F32), logf, preferred_element_type=F32,
                   precision=lax.Precision.HIGHEST)
    logi_t = logi.T
    bcum_t = bcum.T

    vm = vm_ref[0]
    outs = []
    for h in range(MLSTM_HEADS):
        hs = slice(h * MLSTM_DH, (h + 1) * MLSTM_DH)
        ch = c16[:, hs]
        qh = jnp.dot(ch, wq_ref[h], preferred_element_type=F32)
        kh = jnp.dot(ch, wk_ref[h], preferred_element_type=F32) * (MLSTM_DH ** -0.5)
        vh = vm[:, hs].astype(BF16)
        q16 = qh.astype(BF16)
        k16 = kh.astype(BF16)

        b_col = bcum[:, MLSTM_HEADS + h:MLSTM_HEADS + h + 1]
        li_col = logi[:, h:h + 1]
        b_row = bcum_t[MLSTM_HEADS + h:MLSTM_HEADS + h + 1, :]
        li_row = logi_t[h:h + 1, :]
        m_prev = m_state[h]
        c_prev = c_state[h]
        n_prev = n_state[h]

        dmat = jnp.where(tril, b_col - b_row + li_row, NEG_INF)
        inter = b_col + m_prev
        m_t = jnp.maximum(inter, jnp.max(dmat, axis=-1, keepdims=True))
        w_intra = jnp.exp(dmat - m_t)
        w_inter = jnp.exp(inter - m_t)
        sc = lax.dot_general(q16, k16, (((1,), (1,)), ((), ())), preferred_element_type=F32) * w_intra
        num = w_inter * jnp.dot(q16, c_prev.astype(BF16), preferred_element_type=F32) \
            + jnp.dot(sc.astype(BF16), vh, preferred_element_type=F32)
        den = w_inter * jnp.sum(qh * n_prev, axis=-1, keepdims=True) + jnp.sum(sc, axis=-1, keepdims=True)
        outs.append(num / jnp.maximum(jnp.abs(den), jnp.exp(-m_t)))

        b_last = b_col[BLK - 1:BLK, :]
        g_row = b_last - b_row + li_row
        g_col = b_last - b_col + li_col
        m_new = jnp.maximum(b_last + m_prev, jnp.max(g_row, axis=-1, keepdims=True))
        decay = jnp.exp(b_last + m_prev - m_new)
        kw = kh * jnp.exp(g_col - m_new)
        c_state[h] = decay * c_prev + jnp.dot(kw.T.astype(BF16), vh, preferred_element_type=F32)
        n_state[h] = decay * n_prev + jnp.sum(kw, axis=0, keepdims=True)
        m_state[h] = m_new

    o_gate = jax.nn.sigmoid(o_ref[0])
    normed = []
    for h in range(MLSTM_HEADS):
        hs = slice(h * MLSTM_DH, (h + 1) * MLSTM_DH)
        hh = outs[h] * o_gate[:, hs]
        mu = jnp.mean(hh, axis=-1, keepdims=True)
        var = jnp.mean(jnp.square(hh - mu), axis=-1, keepdims=True)
        normed.append((hh - mu) * lax.rsqrt(var + EPS))
    hn = jnp.concatenate(normed, axis=-1)
    y_ref[0] = hn * gn_ref[...] + skip_ref[...] * c


def _mlstm(u, vm, o_pre, gates, conv_w, conv_b, wq, wk, gate_bias, skip, gn_g):
    b, l, _ = u.shape
    blk = lambda bi, ci: (bi, ci, 0)
    c2 = lambda bi, ci: (0, 0)
    c3 = lambda bi, ci: (0, 0, 0)
    t512 = pl.BlockSpec((1, BLK, D_MLSTM), blk)
    return pl.pallas_call(
        _mlstm_kernel,
        grid=(b, l // BLK),
        in_specs=[t512, t512, t512, pl.BlockSpec((1, BLK, GATE_LANES), blk),
                  pl.BlockSpec((CONV_W, D_MLSTM), c2), pl.BlockSpec((1, D_MLSTM), c2),
                  pl.BlockSpec(wq.shape, c3), pl.BlockSpec(wk.shape, c3),
                  pl.BlockSpec((1, GATE_LANES), c2), pl.BlockSpec((1, D_MLSTM), c2),
                  pl.BlockSpec((1, D_MLSTM), c2)],
        out_specs=pl.BlockSpec((1, BLK, D_MLSTM), lambda bi, ci: (bi, jnp.maximum(ci - LEAD // BLK, 0), 0)),
        out_shape=jax.ShapeDtypeStruct((b, l - LEAD, D_MLSTM), F32),
        scratch_shapes=[pltpu.VMEM((MLSTM_HEADS, MLSTM_DH, MLSTM_DH), F32),
                        pltpu.VMEM((MLSTM_HEADS, 1, MLSTM_DH), F32),
                        pltpu.VMEM((MLSTM_HEADS, 1, 1), F32),
                        pltpu.VMEM((8, D_MLSTM), F32)],
        compiler_params=pltpu.CompilerParams(dimension_semantics=("parallel", "arbitrary"),
                                             vmem_limit_bytes=VMEM_LIMIT),
        name="mlstm",
    )(u, vm, o_pre, gates, conv_w, conv_b, wq, wk, gate_bias, skip, gn_g)


PAIR = 2 * SB_DH
N_PAIRS = SB_HEADS // 2


def _sb_kernel(q_ref, k_ref, v_ref, g_ref, o_ref, run_ref, acc_ref):
    qblk = pl.program_id(1) + LEAD // BLK
    row = lax.broadcasted_iota(jnp.int32, (BLK, BLK), 0)
    col = lax.broadcasted_iota(jnp.int32, (BLK, BLK), 1)
    odd_lane = col >= SB_DH
    jj = lax.broadcasted_iota(jnp.int32, (BLK, 2 * BLK), 0)
    ss = lax.broadcasted_iota(jnp.int32, (BLK, 2 * BLK), 1)
    tri2 = jnp.logical_or(jj > ss, ss >= BLK).astype(BF16)

    qf = q_ref[0].astype(F32) * (SB_DH ** -0.5)
    qm = []
    for h in range(SB_HEADS):
        qp = qf[:, (h // 2) * PAIR:(h // 2 + 1) * PAIR]
        qm.append(jnp.where(odd_lane if h % 2 else jnp.logical_not(odd_lane), qp, 0.0).astype(BF16))

    q_pairs = [jnp.concatenate([qm[2 * p], qm[2 * p + 1]], axis=0) for p in range(N_PAIRS)]

    def key_block(start, mask, first):
        kblk = k_ref[0, pl.ds(start, BLK), :]
        vblk = v_ref[0, pl.ds(start, BLK), :]
        z = jnp.concatenate(
            [lax.dot_general(q_pairs[p], kblk[:, p * PAIR:(p + 1) * PAIR], (((1,), (1,)), ((), ())),
                             preferred_element_type=F32) for p in range(N_PAIRS)], axis=0)
        z = z.reshape(SB_HEADS, BLK, BLK)
        lneg = jnp.where(mask[None], -_softplus(z), 0.0)
        flat = lneg.reshape(SB_HEADS * BLK, BLK)
        hi = flat.astype(BF16)
        lo = (flat - hi.astype(F32)).astype(BF16)
        rt = jnp.dot(hi, tri2, preferred_element_type=F32) + jnp.dot(lo, tri2, preferred_element_type=F32)
        rex = rt[:, :BLK].reshape(SB_HEADS, BLK, BLK)
        tot = rt[:, BLK:].reshape(SB_HEADS, BLK, BLK)
        e = z + lneg + rex
        if not first:
            e = e + run_ref[...]
        a = jnp.where(mask[None], jnp.exp(e), 0.0).astype(BF16)
        run_ref[...] = tot if first else run_ref[...] + tot
        for p in range(N_PAIRS):
            vp = vblk[:, p * PAIR:(p + 1) * PAIR]
            zero = jnp.zeros_like(vp)
            v_heads = jnp.concatenate([jnp.where(odd_lane, zero, vp), jnp.where(odd_lane, vp, zero)], axis=0)
            pv = jnp.dot(jnp.concatenate([a[2 * p], a[2 * p + 1]], axis=1), v_heads,
                         preferred_element_type=F32)
            ps = slice(p * PAIR, (p + 1) * PAIR)
            acc_ref[:, ps] = pv if first else acc_ref[:, ps] + pv

    def alive():
        m = run_ref[0]
        for h in range(1, SB_HEADS):
            m = jnp.maximum(m, run_ref[h])
        return (jnp.max(m) > SB_DEAD_LOG).astype(jnp.int32)

    key_block(pl.multiple_of(qblk * BLK, BLK), col < row, True)

    def cond(carry):
        kb, go = carry
        return jnp.logical_and(kb >= 0, go > 0)

    def body(carry):
        kb, _ = carry
        start = pl.multiple_of(kb * BLK, BLK)
        key_block(start, start + col >= PAD, False)
        return kb - 1, alive()

    lax.while_loop(cond, body, (qblk - 1, alive()))

    for p in range(N_PAIRS):
        ps = slice(p * PAIR, (p + 1) * PAIR)
        x = acc_ref[:, ps]
        x2 = x * x
        ms_even = jnp.sum(jnp.where(odd_lane, 0.0, x2), axis=-1, keepdims=True) * (1.0 / SB_DH)
        ms_odd = jnp.sum(jnp.where(odd_lane, x2, 0.0), axis=-1, keepdims=True) * (1.0 / SB_DH)
        scale = jnp.where(odd_lane, lax.rsqrt(ms_odd + EPS), lax.rsqrt(ms_even + EPS))
        o_ref[0, :, ps] = x * scale * g_ref[:, ps]


def _sb(q, k, v, norm_g):
    b, l, d = q.shape
    nq = (l - LEAD) // BLK
    full = pl.BlockSpec((1, l, d), lambda bi, qi: (bi, 0, 0), pipeline_mode=pl.Buffered(1))
    return pl.pallas_call(
        _sb_kernel,
        grid=(b, nq),
        in_specs=[pl.BlockSpec((1, BLK, d), lambda bi, qi: (bi, qi + LEAD // BLK, 0)),
                  full, full, pl.BlockSpec((1, d), lambda bi, qi: (0, 0))],
        out_specs=pl.BlockSpec((1, BLK, d), lambda bi, qi: (bi, qi, 0)),
        out_shape=jax.ShapeDtypeStruct((b, l - LEAD, d), F32),
        scratch_shapes=[pltpu.VMEM((SB_HEADS, BLK, BLK), F32), pltpu.VMEM((BLK, d), F32)],
        compiler_params=pltpu.CompilerParams(dimension_semantics=("parallel", "arbitrary"),
                                             vmem_limit_bytes=VMEM_LIMIT),
        name="stickbreak",
    )(q, k, v, norm_g)


def _outproj_kernel(x_ref, ym_ref, ys_ref, wm_ref, ws_ref, g_ref, h_ref, n_ref):
    h = x_ref[0] + jnp.dot(ym_ref[0].astype(BF16), wm_ref[...], preferred_element_type=F32) \
        + jnp.dot(ys_ref[0].astype(BF16), ws_ref[...], preferred_element_type=F32)
    h_ref[0] = h
    n_ref[0] = (h * _rms_scale(h) * g_ref[...]).astype(BF16)


def _outproj(x, ym, ys, w_m, w_s, g, tm=512):
    b, s, d = x.shape
    blk = lambda bi, i: (bi, i, 0)
    c2 = lambda bi, i: (0, 0)
    return pl.pallas_call(
        _outproj_kernel,
        grid=(b, s // tm),
        in_specs=[pl.BlockSpec((1, tm, d), blk), pl.BlockSpec((1, tm, D_MLSTM), blk),
                  pl.BlockSpec((1, tm, D_SB), blk),
                  pl.BlockSpec(w_m.shape, c2), pl.BlockSpec(w_s.shape, c2), pl.BlockSpec((1, d), c2)],
        out_specs=[pl.BlockSpec((1, tm, d), blk), pl.BlockSpec((1, tm, d), blk)],
        out_shape=[jax.ShapeDtypeStruct((b, s, d), F32), jax.ShapeDtypeStruct((b, s, d), BF16)],
        compiler_params=pltpu.CompilerParams(dimension_semantics=("parallel", "parallel"),
                                             vmem_limit_bytes=VMEM_LIMIT),
        name="outproj",
    )(x, ym, ys, w_m, w_s, g)


LOG2E = 1.4426950408889634


def _top_values(s, k):
    vals = []
    for _ in range(k):
        m = jnp.max(s, axis=0, keepdims=True)
        vals.append(m)
        s = jnp.where(s == m, NEG_INF, s)
    return vals


def _exchange(a, i, l, descending):
    hi = jnp.maximum(a[i], a[l])
    lo = jnp.minimum(a[i], a[l])
    a[i], a[l] = (hi, lo) if descending else (lo, hi)


def _bitonic_merge(a):
    j = len(a) // 2
    while j >= 1:
        for i in range(len(a)):
            if i ^ j > i:
                _exchange(a, i, i ^ j, True)
        j //= 2


def _sorted_top16(sc):
    n = PEER_TOPK
    a = [sc[SUBLANES * g:SUBLANES * (g + 1), :] for g in range(n)]
    k = 2
    while k <= n:
        j = k // 2
        while j >= 1:
            for i in range(n):
                if i ^ j > i:
                    _exchange(a, i, i ^ j, (i & k) == 0)
            j //= 2
        k *= 2
    shift = SUBLANES // 2
    while shift >= 1:
        b = [pltpu.roll(x, shift, axis=0) for x in a]
        a = [jnp.maximum(a[i], b[n - 1 - i]) for i in range(n)]
        _bitonic_merge(a)
        shift //= 2
    return [x[0:1, :] for x in a]


def _pair_candidates(top1, top2):
    cand = []
    for r1 in range(PEER_TOPK):
        cand.append(top1[r1] + jnp.concatenate(top2[:PEER_TOPK // (r1 + 1)], axis=0))
    return jnp.concatenate(cand, axis=0)


def _route_kernel(n_ref, wq_ref, keys_ref, s1_ref, s2_ref, tau_ref):
    qt = lax.dot_general(wq_ref[...], n_ref[...], (((1,), (1,)), ((), ())), preferred_element_type=F32)
    for h in range(PEER_HEADS):
        shifted = []
        tops = []
        for p in range(2):
            r0 = (h * 2 + p) * PEER_HALF
            sc = jnp.dot(keys_ref[p], qt[r0:r0 + PEER_HALF, :].astype(BF16), preferred_element_type=F32)
            top = _sorted_top16(sc)
            shifted.append((sc - top[0]) * LOG2E)
            tops.append([(t - top[0]) * LOG2E for t in top])
        cand = _pair_candidates(tops[0], tops[1])
        tau = _top_values(cand, PEER_TOPK)[-1]
        z = jnp.sum(jnp.where(cand >= tau, jnp.exp2(cand), 0.0), axis=0, keepdims=True)
        log2z = jnp.log(z) * LOG2E
        cand = _pair_candidates(tops[0], [t - log2z for t in tops[1]])
        s1_ref[h] = shifted[0]
        s2_ref[h] = shifted[1] - log2z
        tau_ref[h] = _top_values(cand, PEER_TOPK)[-1]


def _route(n2d, wq_t, keys16, tb=256):
    m = n2d.shape[0]
    big = pl.BlockSpec((PEER_HEADS, N_KEYS, tb), lambda i: (0, 0, i))
    small = pl.BlockSpec((PEER_HEADS, 1, tb), lambda i: (0, 0, i))
    return pl.pallas_call(
        _route_kernel,
        grid=(m // tb,),
        in_specs=[pl.BlockSpec((tb, D_MODEL), lambda i: (i, 0)),
                  pl.BlockSpec(wq_t.shape, lambda i: (0, 0)),
                  pl.BlockSpec(keys16.shape, lambda i: (0, 0, 0))],
        out_specs=[big, big, small],
        out_shape=[jax.ShapeDtypeStruct((PEER_HEADS, N_KEYS, m), F32)] * 2
        + [jax.ShapeDtypeStruct((PEER_HEADS, 1, m), F32)],
        compiler_params=pltpu.CompilerParams(dimension_semantics=("parallel",),
                                             vmem_limit_bytes=VMEM_LIMIT),
        name="peer_route",
    )(n2d, wq_t, keys16)


def _expert_kernel(n_ref, u_ref, v_ref, s1_ref, s2_ref, tau_ref, h_ref, g_ref, o_ref, acc_ref, act_ref,
                   wa_ref):
    ei = pl.program_id(1)
    eb, tb = act_ref.shape

    @pl.when(ei == 0)
    def _():
        acc_ref[...] = jnp.zeros_like(acc_ref)

    chunk = 2 * N_KEYS
    n_chunks = eb // chunk

    tok_w = 256
    out_w = 256

    def activations(c, p):
        rows = slice(c * chunk, (c + 1) * chunk)
        cols = slice(p * tok_w, (p + 1) * tok_w)
        a = lax.dot_general(u_ref[rows, :], n_ref[cols, :], (((1,), (1,)), ((), ())),
                            preferred_element_type=F32)
        act_ref[rows, cols] = 0.5 * a * (1.0 + lax.erf(a * (2.0 ** -0.5)))

    def gates(r, t):
        rs = slice(r * N_KEYS, (r + 1) * N_KEYS)
        ts = slice(t * 128, (t + 1) * 128)
        gate = None
        for h in range(PEER_HEADS):
            pair = s1_ref[h, r:r + 1, ts] + s2_ref[h, :, ts]
            wgt = jnp.where(pair >= tau_ref[h, :, ts], jnp.exp2(pair), 0.0)
            gate = wgt if gate is None else gate + wgt
        wa_ref[rs, ts] = (gate * act_ref[rs, ts]).astype(BF16)

    def outputs(c, p):
        rows = slice(c * chunk, (c + 1) * chunk)
        cols = slice(p * out_w, (p + 1) * out_w)
        acc_ref[:, cols] += lax.dot_general(wa_ref[rows, :], v_ref[rows, cols], (((0,), (0,)), ((), ())),
                                            preferred_element_type=F32)

    tiles = [(r, t) for r in range(chunk // N_KEYS) for t in range(tb // 128)]
    for stage in range(-1, n_chunks + 1):
        mxu_work = []
        if 0 <= stage + 1 < n_chunks:
            mxu_work += [functools.partial(activations, stage + 1, p) for p in range(tb // tok_w)]
        if 0 <= stage - 1 < n_chunks:
            mxu_work += [functools.partial(outputs, stage - 1, p) for p in range(D_MODEL // out_w)]
        vpu_work = []
        if 0 <= stage < n_chunks:
            vpu_work = [functools.partial(gates, stage * (chunk // N_KEYS) + r, t) for r, t in tiles]
        while mxu_work or vpu_work:
            if vpu_work:
                vpu_work.pop(0)()
            if mxu_work:
                mxu_work.pop(0)()

    @pl.when(ei == pl.num_programs(1) - 1)
    def _():
        h = h_ref[...] + acc_ref[...]
        o_ref[...] = h * _rms_scale(h) * g_ref[...]


def _experts(n2d, u16, v16, s1, s2, tau, h2d, g, tb=512, eb=2048):
    m = n2d.shape[0]
    ne = u16.shape[0]
    rows = eb // N_KEYS
    tok = lambda ti, ei: (ti, 0)
    exp = lambda ti, ei: (ei, 0)
    return pl.pallas_call(
        _expert_kernel,
        grid=(m // tb, ne // eb),
        in_specs=[pl.BlockSpec((tb, D_MODEL), tok),
                  pl.BlockSpec((eb, D_MODEL), exp),
                  pl.BlockSpec((eb, D_MODEL), exp),
                  pl.BlockSpec((PEER_HEADS, rows, tb), lambda ti, ei: (0, ei, ti)),
                  pl.BlockSpec((PEER_HEADS, N_KEYS, tb), lambda ti, ei: (0, 0, ti)),
                  pl.BlockSpec((PEER_HEADS, 1, tb), lambda ti, ei: (0, 0, ti)),
                  pl.BlockSpec((tb, D_MODEL), tok),
                  pl.BlockSpec((1, D_MODEL), lambda ti, ei: (0, 0))],
        out_specs=pl.BlockSpec((tb, D_MODEL), tok),
        out_shape=jax.ShapeDtypeStruct((m, D_MODEL), F32),
        scratch_shapes=[pltpu.VMEM((tb, D_MODEL), F32), pltpu.VMEM((eb, tb), F32),
                        pltpu.VMEM((eb, tb), BF16)],
        compiler_params=pltpu.CompilerParams(dimension_semantics=("parallel", "arbitrary"),
                                             vmem_limit_bytes=VMEM_LIMIT),
        name="peer_experts",
    )(n2d, u16, v16, s1, s2, tau, h2d, g)


def kernel(x, meta, norm_mix_g, w_in, conv_w, conv_b, w_qm, w_km, b_i, b_f, mlstm_skip, mlstm_gn_g,
           sb_norm_g, w_out, norm_ffn_g, w_query, sub_keys, expert_u, expert_v, norm_final_g):
    b, s, d = x.shape
    depth = w_in.shape[0]
    assert depth == 1 and d == D_MODEL and s % 512 == 0
    l = s + LEAD
    lead = jnp.concatenate([jnp.zeros((PAD, d), x.dtype), meta.astype(x.dtype)], axis=0)
    h = jnp.concatenate([jnp.broadcast_to(lead[None], (b, LEAD, d)), x], axis=1)

    w = w_in[0]
    o_u, o_v, o_o = 0, D_MLSTM, 2 * D_MLSTM
    o_i = 3 * D_MLSTM
    o_f = o_i + MLSTM_HEADS
    o_q = o_f + MLSTM_HEADS
    w_main = jnp.concatenate([w[:, o_u:o_i], w[:, o_q:]], axis=1).astype(BF16)
    w_gate = jnp.pad(w[:, o_i:o_q], ((0, 0), (0, GATE_LANES - 2 * MLSTM_HEADS))).astype(BF16)
    gate_bias = jnp.pad(jnp.concatenate([b_i[0], b_f[0]]), (0, GATE_LANES - 2 * MLSTM_HEADS))[None, :]

    u, vm, o_pre, gates, qs, ks, vs = _inproj(h.reshape(b * l, d), norm_mix_g[0][None, :], w_main, w_gate)

    r3 = lambda t: t.reshape(b, l, -1)
    ym = _mlstm(r3(u), r3(vm), r3(o_pre), r3(gates), conv_w[0], conv_b[0][None, :],
                w_qm[0].astype(BF16), w_km[0].astype(BF16), gate_bias,
                mlstm_skip[0][None, :], mlstm_gn_g[0][None, :])

    ys = _sb(r3(qs), r3(ks), r3(vs), sb_norm_g[0][None, :])

    wo = w_out[0].astype(BF16)
    h2, n2 = _outproj(x, ym, ys, wo[:D_MLSTM], wo[D_MLSTM:], norm_ffn_g[0][None, :])

    n2d = n2.reshape(b * s, d)
    s1, s2, tau = _route(n2d, w_query[0].T.astype(BF16), sub_keys[0].astype(BF16))
    out = _experts(n2d, expert_u[0].astype(BF16), expert_v[0].astype(BF16), s1, s2, tau,
                   h2.reshape(b * s, d), norm_final_g[None, :])
    return out.reshape(b, s, d)
```

```python
import functools

import jax
import jax.numpy as jnp
from jax import lax
from jax.experimental import pallas as pl
from jax.experimental.pallas import tpu as pltpu

F32 = jnp.float32
BF16 = jnp.bfloat16

D_MODEL = 1024
D_MLSTM = 512
D_SB = 512
MLSTM_HEADS = 4
MLSTM_DH = 128
SB_HEADS = 8
SB_DH = 64
CONV_W = 4
N_META = 16
LEAD = 128
PAD = LEAD - N_META
EPS = 1e-6

PEER_HEADS = 8
N_KEYS = 128
PEER_TOPK = 16
PEER_HALF = 128

SUBLANES = 8
GATE_LANES = 128
BLK = 128
NEG_INF = float("-inf")
SB_DEAD_LOG = -105.0

VMEM_LIMIT = 56 * 1024 * 1024


def _softplus(z):
    return jnp.maximum(z, 0.0) + jnp.log1p(jnp.exp(-jnp.abs(z)))


def _rms_scale(x):
    return lax.rsqrt(jnp.mean(x * x, axis=-1, keepdims=True) + EPS)


def _inproj_kernel(h_ref, g_ref, wm_ref, wg_ref, u_ref, vm_ref, o_ref, gate_ref, q_ref, k_ref, v_ref):
    x = h_ref[...]
    n = (x * _rms_scale(x) * g_ref[...]).astype(BF16)
    y = jnp.dot(n, wm_ref[...], preferred_element_type=F32)
    u_ref[...] = y[:, 0 * 512:1 * 512]
    vm_ref[...] = y[:, 1 * 512:2 * 512]
    o_ref[...] = y[:, 2 * 512:3 * 512]
    q_ref[...] = y[:, 3 * 512:4 * 512].astype(BF16)
    k_ref[...] = y[:, 4 * 512:5 * 512].astype(BF16)
    v_ref[...] = y[:, 5 * 512:6 * 512].astype(BF16)
    gate_ref[...] = jnp.dot(n, wg_ref[...], preferred_element_type=F32)


def _inproj(h2d, g, w_main, w_gate, tm=256):
    m = h2d.shape[0]
    row = lambda i: (i, 0)
    const = lambda i: (0, 0)
    o512 = pl.BlockSpec((tm, 512), row)
    return pl.pallas_call(
        _inproj_kernel,
        grid=(m // tm,),
        in_specs=[pl.BlockSpec((tm, D_MODEL), row), pl.BlockSpec((1, D_MODEL), const),
                  pl.BlockSpec(w_main.shape, const), pl.BlockSpec(w_gate.shape, const)],
        out_specs=[o512, o512, o512, pl.BlockSpec((tm, GATE_LANES), row), o512, o512, o512],
        out_shape=[jax.ShapeDtypeStruct((m, 512), F32)] * 3
        + [jax.ShapeDtypeStruct((m, GATE_LANES), F32)]
        + [jax.ShapeDtypeStruct((m, 512), BF16)] * 3,
        compiler_params=pltpu.CompilerParams(dimension_semantics=("parallel",),
                                             vmem_limit_bytes=VMEM_LIMIT),
        name="inproj",
    )(h2d, g, w_main, w_gate)


def _mlstm_kernel(u_ref, vm_ref, o_ref, gate_ref, gate_t_ref, cw_ref, cb_ref, wq_ref, wkt_ref, gb_ref, gbt_ref,
                  skip_ref, gn_ref, y_ref, c_state, n_state, m_state, u_tail):
    ci = pl.program_id(1)

    @pl.when(ci == 0)
    def _():
        c_state[...] = jnp.zeros_like(c_state)
        n_state[...] = jnp.zeros_like(n_state)
        m_state[...] = jnp.zeros_like(m_state)
        u_tail[...] = jnp.zeros_like(u_tail)

    tpos = ci * BLK + lax.broadcasted_iota(jnp.int32, (BLK, 1), 0)
    valid = tpos >= PAD
    valid_row = ci * BLK + lax.broadcasted_iota(jnp.int32, (1, BLK), 1) >= PAD

    u = jnp.where(valid, u_ref[0], 0.0)
    ext = jnp.concatenate([u_tail[...], u], axis=0)
    u_tail[...] = u[BLK - 8:, :]
    cw = cw_ref[...]
    c = cb_ref[...] + cw[CONV_W - 1:CONV_W, :] * u
    for j in range(CONV_W - 1):
        off = 8 - (CONV_W - 1) + j
        c = c + cw[j:j + 1, :] * ext[off:off + BLK, :]
    c = c * jax.nn.sigmoid(c)
    c16 = c.astype(BF16)

    gpre = gate_ref[0] + gb_ref[...]
    logf = jnp.where(valid, -_softplus(-gpre), 0.0)
    gpre_t = gate_t_ref[0] + gbt_ref[...]
    logi_t = jnp.where(valid_row, gpre_t, NEG_INF)
    logf_t = jnp.where(valid_row, -_softplus(-gpre_t), 0.0)
    r = lax.broadcasted_iota(jnp.int32, (BLK, BLK), 0)
    s = lax.broadcasted_iota(jnp.int32, (BLK, BLK), 1)
    tril = s <= r
    bcum = jnp.dot(tril.astype(F32), logf, preferred_element_type=F32,
                   precision=lax.Precision.HIGHEST)
    bcum_t = jnp.dot(logf_t, (r <= s).astype(F32), preferred_element_type=F32,
                     precision=lax.Precision.HIGHEST)

    q_all = jnp.dot(c16, wq_ref[...], preferred_element_type=F32)
    k_t = lax.dot_general(wkt_ref[...], c16, (((1,), (1,)), ((), ())),
                          preferred_element_type=F32) * (MLSTM_DH ** -0.5)
    q16 = q_all.astype(BF16)
    kt16 = k_t.astype(BF16)
    ones16 = jnp.ones((SUBLANES, BLK), BF16)

    vm = vm_ref[0]
    outs = []
    for h in range(MLSTM_HEADS):
        hs = slice(h * MLSTM_DH, (h + 1) * MLSTM_DH)
        vh = vm[:, hs].astype(BF16)
        b_col = bcum[:, MLSTM_HEADS + h:MLSTM_HEADS + h + 1]
        b_row = bcum_t[MLSTM_HEADS + h:MLSTM_HEADS + h + 1, :]
        li_row = logi_t[h:h + 1, :]
        m_prev = m_state[h]
        c_prev = c_state[h]
        n_prev = n_state[h]

        dmat = jnp.where(tril, b_col - b_row + li_row, NEG_INF)
        inter = b_col + m_prev
        m_t = jnp.maximum(inter, jnp.max(dmat, axis=-1, keepdims=True))
        w_intra = jnp.exp(dmat - m_t)
        w_inter = jnp.exp(inter - m_t)
        sc = jnp.dot(q16[:, hs], kt16[hs, :], preferred_element_type=F32) * w_intra
        num = w_inter * jnp.dot(q16[:, hs], c_prev.astype(BF16), preferred_element_type=F32) \
            + jnp.dot(sc.astype(BF16), vh, preferred_element_type=F32)
        den = w_inter * jnp.sum(q_all[:, hs] * n_prev, axis=-1, keepdims=True) \
            + jnp.sum(sc, axis=-1, keepdims=True)
        outs.append(num / jnp.maximum(jnp.abs(den), jnp.exp(-m_t)))

        b_last = b_row[:, BLK - 1:BLK]
        g_row = b_last - b_row + li_row
        m_new = jnp.maximum(b_last + m_prev, jnp.max(g_row, axis=-1, keepdims=True))
        decay = jnp.exp(b_last + m_prev - m_new)
        kwt = (k_t[hs, :] * jnp.exp(g_row - m_new)).astype(BF16)
        c_state[h] = decay * c_prev + jnp.dot(kwt, vh, preferred_element_type=F32)
        n_sum = lax.dot_general(ones16, kwt, (((1,), (1,)), ((), ())), preferred_element_type=F32)
        n_state[h] = decay * n_prev + n_sum[0:1, :]
        m_state[h] = m_new

    o_gate = jax.nn.sigmoid(o_ref[0])
    normed = []
    for h in range(MLSTM_HEADS):
        hs = slice(h * MLSTM_DH, (h + 1) * MLSTM_DH)
        hh = outs[h] * o_gate[:, hs]
        mu = jnp.mean(hh, axis=-1, keepdims=True)
        var = jnp.mean(jnp.square(hh - mu), axis=-1, keepdims=True)
        normed.append((hh - mu) * lax.rsqrt(var + EPS))
    hn = jnp.concatenate(normed, axis=-1)
    y_ref[0] = hn * gn_ref[...] + skip_ref[...] * c


def _mlstm(u, vm, o_pre, gates, gates_t, conv_w, conv_b, wq_bd, wkt_bd, gate_bias, skip, gn_g):
    b, l, _ = u.shape
    blk = lambda bi, ci: (bi, ci, 0)
    c2 = lambda bi, ci: (0, 0)
    t512 = pl.BlockSpec((1, BLK, D_MLSTM), blk)
    return pl.pallas_call(
        _mlstm_kernel,
        grid=(b, l // BLK),
        in_specs=[t512, t512, t512, pl.BlockSpec((1, BLK, GATE_LANES), blk),
                  pl.BlockSpec((1, GATE_LANES, BLK), lambda bi, ci: (bi, 0, ci)),
                  pl.BlockSpec((CONV_W, D_MLSTM), c2), pl.BlockSpec((1, D_MLSTM), c2),
                  pl.BlockSpec(wq_bd.shape, c2), pl.BlockSpec(wkt_bd.shape, c2),
                  pl.BlockSpec((1, GATE_LANES), c2), pl.BlockSpec((GATE_LANES, 1), c2),
                  pl.BlockSpec((1, D_MLSTM), c2), pl.BlockSpec((1, D_MLSTM), c2)],
        out_specs=pl.BlockSpec((1, BLK, D_MLSTM), lambda bi, ci: (bi, jnp.maximum(ci - LEAD // BLK, 0), 0)),
        out_shape=jax.ShapeDtypeStruct((b, l - LEAD, D_MLSTM), F32),
        scratch_shapes=[pltpu.VMEM((MLSTM_HEADS, MLSTM_DH, MLSTM_DH), F32),
                        pltpu.VMEM((MLSTM_HEADS, 1, MLSTM_DH), F32),
                        pltpu.VMEM((MLSTM_HEADS, 1, 1), F32),
                        pltpu.VMEM((8, D_MLSTM), F32)],
        compiler_params=pltpu.CompilerParams(dimension_semantics=("parallel", "arbitrary"),
                                             vmem_limit_bytes=VMEM_LIMIT),
        name="mlstm",
    )(u, vm, o_pre, gates, gates_t, conv_w, conv_b, wq_bd, wkt_bd, gate_bias, gate_bias.T, skip, gn_g)


PAIR = 2 * SB_DH
N_PAIRS = SB_HEADS // 2


def _sb_kernel(q_ref, k_ref, v_ref, g_ref, o_ref, run_ref, acc_ref):
    qblk = pl.program_id(1) + LEAD // BLK
    row = lax.broadcasted_iota(jnp.int32, (BLK, BLK), 0)
    col = lax.broadcasted_iota(jnp.int32, (BLK, BLK), 1)
    odd_lane = col >= SB_DH
    jj = lax.broadcasted_iota(jnp.int32, (BLK, 2 * BLK), 0)
    ss = lax.broadcasted_iota(jnp.int32, (BLK, 2 * BLK), 1)
    tri2 = jnp.logical_or(jj > ss, ss >= BLK).astype(BF16)

    qf = q_ref[0].astype(F32) * (SB_DH ** -0.5)
    qm = []
    for h in range(SB_HEADS):
        qp = qf[:, (h // 2) * PAIR:(h // 2 + 1) * PAIR]
        qm.append(jnp.where(odd_lane if h % 2 else jnp.logical_not(odd_lane), qp, 0.0).astype(BF16))

    q_pairs = [jnp.concatenate([qm[2 * p], qm[2 * p + 1]], axis=0) for p in range(N_PAIRS)]

    def key_block(start, mask, first):
        kblk = k_ref[0, pl.ds(start, BLK), :]
        vblk = v_ref[0, pl.ds(start, BLK), :]
        z = jnp.concatenate(
            [lax.dot_general(q_pairs[p], kblk[:, p * PAIR:(p + 1) * PAIR], (((1,), (1,)), ((), ())),
                             preferred_element_type=F32) for p in range(N_PAIRS)], axis=0)
        z = z.reshape(SB_HEADS, BLK, BLK)
        lneg = jnp.where(mask[None], -_softplus(z), 0.0)
        flat = lneg.reshape(SB_HEADS * BLK, BLK)
        hi = flat.astype(BF16)
        lo = (flat - hi.astype(F32)).astype(BF16)
        rt = jnp.dot(hi, tri2, preferred_element_type=F32) + jnp.dot(lo, tri2, preferred_element_type=F32)
        rex = rt[:, :BLK].reshape(SB_HEADS, BLK, BLK)
        tot = rt[:, BLK:].reshape(SB_HEADS, BLK, BLK)
        e = z + lneg + rex
        if not first:
            e = e + run_ref[...]
        a = jnp.where(mask[None], jnp.exp(e), 0.0).astype(BF16)
        run_ref[...] = tot if first else run_ref[...] + tot
        for p in range(N_PAIRS):
            vp = vblk[:, p * PAIR:(p + 1) * PAIR]
            zero = jnp.zeros_like(vp)
            v_heads = jnp.concatenate([jnp.where(odd_lane, zero, vp), jnp.where(odd_lane, vp, zero)], axis=0)
            pv = jnp.dot(jnp.concatenate([a[2 * p], a[2 * p + 1]], axis=1), v_heads,
                         preferred_element_type=F32)
            ps = slice(p * PAIR, (p + 1) * PAIR)
            acc_ref[:, ps] = pv if first else acc_ref[:, ps] + pv

    def alive():
        m = run_ref[0]
        for h in range(1, SB_HEADS):
            m = jnp.maximum(m, run_ref[h])
        return (jnp.max(m) > SB_DEAD_LOG).astype(jnp.int32)

    key_block(pl.multiple_of(qblk * BLK, BLK), col < row, True)

    def cond(carry):
        kb, go = carry
        return jnp.logical_and(kb >= 0, go > 0)

    def body(carry):
        kb, _ = carry
        start = pl.multiple_of(kb * BLK, BLK)
        key_block(start, start + col >= PAD, False)
        return kb - 1, alive()

    lax.while_loop(cond, body, (qblk - 1, alive()))

    for p in range(N_PAIRS):
        ps = slice(p * PAIR, (p + 1) * PAIR)
        x = acc_ref[:, ps]
        x2 = x * x
        ms_even = jnp.sum(jnp.where(odd_lane, 0.0, x2), axis=-1, keepdims=True) * (1.0 / SB_DH)
        ms_odd = jnp.sum(jnp.where(odd_lane, x2, 0.0), axis=-1, keepdims=True) * (1.0 / SB_DH)
        scale = jnp.where(odd_lane, lax.rsqrt(ms_odd + EPS), lax.rsqrt(ms_even + EPS))
        o_ref[0, :, ps] = x * scale * g_ref[:, ps]


def _sb(q, k, v, norm_g):
    b, l, d = q.shape
    nq = (l - LEAD) // BLK
    full = pl.BlockSpec((1, l, d), lambda bi, qi: (bi, 0, 0), pipeline_mode=pl.Buffered(1))
    return pl.pallas_call(
        _sb_kernel,
        grid=(b, nq),
        in_specs=[pl.BlockSpec((1, BLK, d), lambda bi, qi: (bi, qi + LEAD // BLK, 0)),
                  full, full, pl.BlockSpec((1, d), lambda bi, qi: (0, 0))],
        out_specs=pl.BlockSpec((1, BLK, d), lambda bi, qi: (bi, qi, 0)),
        out_shape=jax.ShapeDtypeStruct((b, l - LEAD, d), F32),
        scratch_shapes=[pltpu.VMEM((SB_HEADS, BLK, BLK), F32), pltpu.VMEM((BLK, d), F32)],
        compiler_params=pltpu.CompilerParams(dimension_semantics=("parallel", "arbitrary"),
                                             vmem_limit_bytes=VMEM_LIMIT),
        name="stickbreak",
    )(q, k, v, norm_g)


def _outproj_kernel(x_ref, ym_ref, ys_ref, wm_ref, ws_ref, g_ref, h_ref, n_ref):
    h = x_ref[0] + jnp.dot(ym_ref[0].astype(BF16), wm_ref[...], preferred_element_type=F32) \
        + jnp.dot(ys_ref[0].astype(BF16), ws_ref[...], preferred_element_type=F32)
    h_ref[0] = h
    n_ref[0] = (h * _rms_scale(h) * g_ref[...]).astype(BF16)


def _outproj(x, ym, ys, w_m, w_s, g, tm=512):
    b, s, d = x.shape
    blk = lambda bi, i: (bi, i, 0)
    c2 = lambda bi, i: (0, 0)
    return pl.pallas_call(
        _outproj_kernel,
        grid=(b, s // tm),
        in_specs=[pl.BlockSpec((1, tm, d), blk), pl.BlockSpec((1, tm, D_MLSTM), blk),
                  pl.BlockSpec((1, tm, D_SB), blk),
                  pl.BlockSpec(w_m.shape, c2), pl.BlockSpec(w_s.shape, c2), pl.BlockSpec((1, d), c2)],
        out_specs=[pl.BlockSpec((1, tm, d), blk), pl.BlockSpec((1, tm, d), blk)],
        out_shape=[jax.ShapeDtypeStruct((b, s, d), F32), jax.ShapeDtypeStruct((b, s, d), BF16)],
        compiler_params=pltpu.CompilerParams(dimension_semantics=("parallel", "parallel"),
                                             vmem_limit_bytes=VMEM_LIMIT),
        name="outproj",
    )(x, ym, ys, w_m, w_s, g)


LOG2E = 1.4426950408889634


def _top_values(s, k):
    vals = []
    for _ in range(k):
        m = jnp.max(s, axis=0, keepdims=True)
        vals.append(m)
        s = jnp.where(s == m, NEG_INF, s)
    return vals


def _exchange(a, i, l, descending):
    hi = jnp.maximum(a[i], a[l])
    lo = jnp.minimum(a[i], a[l])
    a[i], a[l] = (hi, lo) if descending else (lo, hi)


def _bitonic_merge(a):
    j = len(a) // 2
    while j >= 1:
        for i in range(len(a)):
            if i ^ j > i:
                _exchange(a, i, i ^ j, True)
        j //= 2


def _sorted_top16(sc):
    n = PEER_TOPK
    a = [sc[SUBLANES * g:SUBLANES * (g + 1), :] for g in range(n)]
    k = 2
    while k <= n:
        j = k // 2
        while j >= 1:
            for i in range(n):
                if i ^ j > i:
                    _exchange(a, i, i ^ j, (i & k) == 0)
            j //= 2
        k *= 2
    shift = SUBLANES // 2
    while shift >= 1:
        b = [pltpu.roll(x, shift, axis=0) for x in a]
        a = [jnp.maximum(a[i], b[n - 1 - i]) for i in range(n)]
        _bitonic_merge(a)
        shift //= 2
    return [x[0:1, :] for x in a]


def _pair_candidates(top1, top2):
    cand = []
    for r1 in range(PEER_TOPK):
        cand.append(top1[r1] + jnp.concatenate(top2[:PEER_TOPK // (r1 + 1)], axis=0))
    return jnp.concatenate(cand, axis=0)


def _route_kernel(n_ref, wq_ref, keys_ref, s1_ref, s2_ref, tau_ref):
    qt = lax.dot_general(wq_ref[...], n_ref[...], (((1,), (1,)), ((), ())), preferred_element_type=F32)
    for h in range(PEER_HEADS):
        shifted = []
        tops = []
        for p in range(2):
            r0 = (h * 2 + p) * PEER_HALF
            sc = jnp.dot(keys_ref[p], qt[r0:r0 + PEER_HALF, :].astype(BF16), preferred_element_type=F32)
            top = _sorted_top16(sc)
            shifted.append((sc - top[0]) * LOG2E)
            tops.append([(t - top[0]) * LOG2E for t in top])
        cand = _pair_candidates(tops[0], tops[1])
        tau = _top_values(cand, PEER_TOPK)[-1]
        z = jnp.sum(jnp.where(cand >= tau, jnp.exp2(cand), 0.0), axis=0, keepdims=True)
        log2z = jnp.log(z) * LOG2E
        cand = _pair_candidates(tops[0], [t - log2z for t in tops[1]])
        s1_ref[h] = shifted[0]
        s2_ref[h] = shifted[1] - log2z
        tau_ref[h] = _top_values(cand, PEER_TOPK)[-1]


def _route(n2d, wq_t, keys16, tb=256):
    m = n2d.shape[0]
    big = pl.BlockSpec((PEER_HEADS, N_KEYS, tb), lambda i: (0, 0, i))
    small = pl.BlockSpec((PEER_HEADS, 1, tb), lambda i: (0, 0, i))
    return pl.pallas_call(
        _route_kernel,
        grid=(m // tb,),
        in_specs=[pl.BlockSpec((tb, D_MODEL), lambda i: (i, 0)),
                  pl.BlockSpec(wq_t.shape, lambda i: (0, 0)),
                  pl.BlockSpec(keys16.shape, lambda i: (0, 0, 0))],
        out_specs=[big, big, small],
        out_shape=[jax.ShapeDtypeStruct((PEER_HEADS, N_KEYS, m), F32)] * 2
        + [jax.ShapeDtypeStruct((PEER_HEADS, 1, m), F32)],
        compiler_params=pltpu.CompilerParams(dimension_semantics=("parallel",),
                                             vmem_limit_bytes=VMEM_LIMIT),
        name="peer_route",
    )(n2d, wq_t, keys16)


def _expert_kernel(n_ref, u_ref, v_ref, s1_ref, s2_ref, tau_ref, h_ref, g_ref, o_ref, acc_ref, act_ref,
                   wa_ref):
    ei = pl.program_id(1)
    eb, tb = act_ref.shape

    @pl.when(ei == 0)
    def _():
        acc_ref[...] = jnp.zeros_like(acc_ref)

    chunk = 2 * N_KEYS
    n_chunks = eb // chunk

    tok_w = 256
    out_w = 512

    def activations(c, p):
        rows = slice(c * chunk, (c + 1) * chunk)
        cols = slice(p * tok_w, (p + 1) * tok_w)
        a = lax.dot_general(u_ref[rows, :], n_ref[cols, :], (((1,), (1,)), ((), ())),
                            preferred_element_type=F32)
        act_ref[rows, cols] = 0.5 * a * (1.0 + lax.erf(a * (2.0 ** -0.5)))

    def gates(r, t):
        rs = slice(r * N_KEYS, (r + 1) * N_KEYS)
        ts = slice(t * 128, (t + 1) * 128)
        gate = None
        for h in range(PEER_HEADS):
            pair = s1_ref[h, r:r + 1, ts] + s2_ref[h, :, ts]
            wgt = jnp.where(pair >= tau_ref[h, :, ts], jnp.exp2(pair), 0.0)
            gate = wgt if gate is None else gate + wgt
        wa_ref[rs, ts] = (gate * act_ref[rs, ts]).astype(BF16)

    def outputs(c, p):
        rows = slice(c * chunk, (c + 1) * chunk)
        cols = slice(p * out_w, (p + 1) * out_w)
        acc_ref[:, cols] += lax.dot_general(wa_ref[rows, :], v_ref[rows, cols], (((0,), (0,)), ((), ())),
                                            preferred_element_type=F32)

    tiles = [(r, t) for r in range(chunk // N_KEYS) for t in range(tb // 128)]
    for stage in range(-1, n_chunks + 1):
        mxu_work = []
        if 0 <= stage + 1 < n_chunks:
            mxu_work += [functools.partial(activations, stage + 1, p) for p in range(tb // tok_w)]
        if 0 <= stage - 1 < n_chunks:
            mxu_work += [functools.partial(outputs, stage - 1, p) for p in range(D_MODEL // out_w)]
        vpu_work = []
        if 0 <= stage < n_chunks:
            vpu_work = [functools.partial(gates, stage * (chunk // N_KEYS) + r, t) for r, t in tiles]
        while mxu_work or vpu_work:
            if vpu_work:
                vpu_work.pop(0)()
            if mxu_work:
                mxu_work.pop(0)()

    @pl.when(ei == pl.num_programs(1) - 1)
    def _():
        h = h_ref[...] + acc_ref[...]
        o_ref[...] = h * _rms_scale(h) * g_ref[...]


def _experts(n2d, u16, v16, s1, s2, tau, h2d, g, tb=512, eb=2048):
    m = n2d.shape[0]
    ne = u16.shape[0]
    rows = eb // N_KEYS
    tok = lambda ti, ei: (ti, 0)
    exp = lambda ti, ei: (ei, 0)
    return pl.pallas_call(
        _expert_kernel,
        grid=(m // tb, ne // eb),
        in_specs=[pl.BlockSpec((tb, D_MODEL), tok),
                  pl.BlockSpec((eb, D_MODEL), exp),
                  pl.BlockSpec((eb, D_MODEL), exp),
                  pl.BlockSpec((PEER_HEADS, rows, tb), lambda ti, ei: (0, ei, ti)),
                  pl.BlockSpec((PEER_HEADS, N_KEYS, tb), lambda ti, ei: (0, 0, ti)),
                  pl.BlockSpec((PEER_HEADS, 1, tb), lambda ti, ei: (0, 0, ti)),
                  pl.BlockSpec((tb, D_MODEL), tok),
                  pl.BlockSpec((1, D_MODEL), lambda ti, ei: (0, 0))],
        out_specs=pl.BlockSpec((tb, D_MODEL), tok),
        out_shape=jax.ShapeDtypeStruct((m, D_MODEL), F32),
        scratch_shapes=[pltpu.VMEM((tb, D_MODEL), F32), pltpu.VMEM((eb, tb), F32),
                        pltpu.VMEM((eb, tb), BF16)],
        compiler_params=pltpu.CompilerParams(dimension_semantics=("parallel", "arbitrary"),
                                             vmem_limit_bytes=VMEM_LIMIT),
        name="peer_experts",
    )(n2d, u16, v16, s1, s2, tau, h2d, g)


def _block_diag(w):
    h, d, _ = w.shape
    eye = jnp.eye(h, dtype=w.dtype)
    return (eye[:, None, :, None] * w[:, :, None, :]).reshape(h * d, h * d)


def kernel(x, meta, norm_mix_g, w_in, conv_w, conv_b, w_qm, w_km, b_i, b_f, mlstm_skip, mlstm_gn_g,
           sb_norm_g, w_out, norm_ffn_g, w_query, sub_keys, expert_u, expert_v, norm_final_g):
    b, s, d = x.shape
    depth = w_in.shape[0]
    assert depth == 1 and d == D_MODEL and s % 512 == 0
    l = s + LEAD
    lead = jnp.concatenate([jnp.zeros((PAD, d), x.dtype), meta.astype(x.dtype)], axis=0)
    h = jnp.concatenate([jnp.broadcast_to(lead[None], (b, LEAD, d)), x], axis=1)

    w = w_in[0]
    o_u, o_v, o_o = 0, D_MLSTM, 2 * D_MLSTM
    o_i = 3 * D_MLSTM
    o_f = o_i + MLSTM_HEADS
    o_q = o_f + MLSTM_HEADS
    w_main = jnp.concatenate([w[:, o_u:o_i], w[:, o_q:]], axis=1).astype(BF16)
    w_gate = jnp.pad(w[:, o_i:o_q], ((0, 0), (0, GATE_LANES - 2 * MLSTM_HEADS))).astype(BF16)
    gate_bias = jnp.pad(jnp.concatenate([b_i[0], b_f[0]]), (0, GATE_LANES - 2 * MLSTM_HEADS))[None, :]

    u, vm, o_pre, gates, qs, ks, vs = _inproj(h.reshape(b * l, d), norm_mix_g[0][None, :], w_main, w_gate)

    r3 = lambda t: t.reshape(b, l, -1)
    gates3 = r3(gates)
    ym = _mlstm(r3(u), r3(vm), r3(o_pre), gates3, gates3.transpose(0, 2, 1), conv_w[0], conv_b[0][None, :],
                _block_diag(w_qm[0]).astype(BF16), _block_diag(w_km[0]).T.astype(BF16), gate_bias,
                mlstm_skip[0][None, :], mlstm_gn_g[0][None, :])

    ys = _sb(r3(qs), r3(ks), r3(vs), sb_norm_g[0][None, :])

    wo = w_out[0].astype(BF16)
    h2, n2 = _outproj(x, ym, ys, wo[:D_MLSTM], wo[D_MLSTM:], norm_ffn_g[0][None, :])

    n2d = n2.reshape(b * s, d)
    s1, s2, tau = _route(n2d, w_query[0].T.astype(BF16), sub_keys[0].astype(BF16))
    out = _experts(n2d, expert_u[0].astype(BF16), expert_v[0].astype(BF16), s1, s2, tau,
                   h2.reshape(b * s, d), norm_final_g[None, :])
    return out.reshape(b, s, d)
```

```python
import functools

import jax
import jax.numpy as jnp
from jax import lax
from jax.experimental import pallas as pl
from jax.experimental.pallas import tpu as pltpu

F32 = jnp.float32
BF16 = jnp.bfloat16

D_MODEL = 1024
D_MLSTM = 512
D_SB = 512
MLSTM_HEADS = 4
MLSTM_DH = 128
SB_HEADS = 8
SB_DH = 64
CONV_W = 4
N_META = 16
LEAD = 128
PAD = LEAD - N_META
EPS = 1e-6

PEER_HEADS = 8
N_KEYS = 128
PEER_TOPK = 16
PEER_HALF = 128

SUBLANES = 8
GATE_LANES = 128
BLK = 128
NEG_INF = float("-inf")
SB_DEAD_LOG = -105.0

VMEM_LIMIT = 56 * 1024 * 1024


def _softplus(z):
    return jnp.maximum(z, 0.0) + jnp.log1p(jnp.exp(-jnp.abs(z)))


def _rms_scale(x):
    return lax.rsqrt(jnp.mean(x * x, axis=-1, keepdims=True) + EPS)


def _inproj_kernel(h_ref, g_ref, wm_ref, wg_ref, u_ref, vm_ref, o_ref, gate_ref, q_ref, k_ref, v_ref):
    x = h_ref[...]
    n = (x * _rms_scale(x) * g_ref[...]).astype(BF16)
    y = jnp.dot(n, wm_ref[...], preferred_element_type=F32)
    u_ref[...] = y[:, 0 * 512:1 * 512]
    vm_ref[...] = y[:, 1 * 512:2 * 512]
    o_ref[...] = y[:, 2 * 512:3 * 512]
    q_ref[...] = y[:, 3 * 512:4 * 512].astype(BF16)
    k_ref[...] = y[:, 4 * 512:5 * 512].astype(BF16)
    v_ref[...] = y[:, 5 * 512:6 * 512].astype(BF16)
    gate_ref[...] = jnp.dot(n, wg_ref[...], preferred_element_type=F32)


def _inproj(h2d, g, w_main, w_gate, tm=256):
    m = h2d.shape[0]
    row = lambda i: (i, 0)
    const = lambda i: (0, 0)
    o512 = pl.BlockSpec((tm, 512), row)
    return pl.pallas_call(
        _inproj_kernel,
        grid=(m // tm,),
        in_specs=[pl.BlockSpec((tm, D_MODEL), row), pl.BlockSpec((1, D_MODEL), const),
                  pl.BlockSpec(w_main.shape, const), pl.BlockSpec(w_gate.shape, const)],
        out_specs=[o512, o512, o512, pl.BlockSpec((tm, GATE_LANES), row), o512, o512, o512],
        out_shape=[jax.ShapeDtypeStruct((m, 512), F32)] * 3
        + [jax.ShapeDtypeStruct((m, GATE_LANES), F32)]
        + [jax.ShapeDtypeStruct((m, 512), BF16)] * 3,
        compiler_params=pltpu.CompilerParams(dimension_semantics=("parallel",),
                                             vmem_limit_bytes=VMEM_LIMIT),
        name="inproj",
    )(h2d, g, w_main, w_gate)


def _mlstm_kernel(u_ref, vm_ref, o_ref, gate_ref, gate_t_ref, cw_ref, cb_ref, wq_ref, wkt_ref, gb_ref, gbt_ref,
                  skip_ref, gn_ref, y_ref, c_state, n_state, m_state, u_tail):
    ci = pl.program_id(1)

    @pl.when(ci == 0)
    def _():
        c_state[...] = jnp.zeros_like(c_state)
        n_state[...] = jnp.zeros_like(n_state)
        m_state[...] = jnp.zeros_like(m_state)
        u_tail[...] = jnp.zeros_like(u_tail)

    tpos = ci * BLK + lax.broadcasted_iota(jnp.int32, (BLK, 1), 0)
    valid = tpos >= PAD
    valid_row = ci * BLK + lax.broadcasted_iota(jnp.int32, (1, BLK), 1) >= PAD

    u = jnp.where(valid, u_ref[0], 0.0)
    ext = jnp.concatenate([u_tail[...], u], axis=0)
    u_tail[...] = u[BLK - 8:, :]
    cw = cw_ref[...]
    c = cb_ref[...] + cw[CONV_W - 1:CONV_W, :] * u
    for j in range(CONV_W - 1):
        off = 8 - (CONV_W - 1) + j
        c = c + cw[j:j + 1, :] * ext[off:off + BLK, :]
    c = c * jax.nn.sigmoid(c)
    c16 = c.astype(BF16)

    gpre = gate_ref[0] + gb_ref[...]
    logf = jnp.where(valid, -_softplus(-gpre), 0.0)
    gpre_t = gate_t_ref[0] + gbt_ref[...]
    logi_t = jnp.where(valid_row, gpre_t, NEG_INF)
    logf_t = jnp.where(valid_row, -_softplus(-gpre_t), 0.0)
    r = lax.broadcasted_iota(jnp.int32, (BLK, BLK), 0)
    s = lax.broadcasted_iota(jnp.int32, (BLK, BLK), 1)
    tril = s <= r
    bcum = jnp.dot(tril.astype(F32), logf, preferred_element_type=F32,
                   precision=lax.Precision.HIGHEST)
    bcum_t = jnp.dot(logf_t, (r <= s).astype(F32), preferred_element_type=F32,
                     precision=lax.Precision.HIGHEST)

    q_all = jnp.dot(c16, wq_ref[...], preferred_element_type=F32)
    k_t = lax.dot_general(wkt_ref[...], c16, (((1,), (1,)), ((), ())),
                          preferred_element_type=F32) * (MLSTM_DH ** -0.5)
    q16 = q_all.astype(BF16)
    kt16 = k_t.astype(BF16)
    ones16 = jnp.ones((SUBLANES, BLK), BF16)

    vm = vm_ref[0]
    outs = []
    for h in range(MLSTM_HEADS):
        hs = slice(h * MLSTM_DH, (h + 1) * MLSTM_DH)
        vh = vm[:, hs].astype(BF16)
        b_col = bcum[:, MLSTM_HEADS + h:MLSTM_HEADS + h + 1]
        b_row = bcum_t[MLSTM_HEADS + h:MLSTM_HEADS + h + 1, :]
        li_row = logi_t[h:h + 1, :]
        m_prev = m_state[h]
        c_prev = c_state[h]
        n_prev = n_state[h]

        dmat = jnp.where(tril, b_col - b_row + li_row, NEG_INF)
        inter = b_col + m_prev
        m_t = jnp.maximum(inter, jnp.max(dmat, axis=-1, keepdims=True))
        w_intra = jnp.exp(dmat - m_t)
        w_inter = jnp.exp(inter - m_t)
        sc = jnp.dot(q16[:, hs], kt16[hs, :], preferred_element_type=F32) * w_intra
        num = w_inter * jnp.dot(q16[:, hs], c_prev.astype(BF16), preferred_element_type=F32) \
            + jnp.dot(sc.astype(BF16), vh, preferred_element_type=F32)
        den = w_inter * jnp.sum(q_all[:, hs] * n_prev, axis=-1, keepdims=True) \
            + jnp.sum(sc, axis=-1, keepdims=True)
        outs.append(num / jnp.maximum(jnp.abs(den), jnp.exp(-m_t)))

        b_last = b_row[:, BLK - 1:BLK]
        g_row = b_last - b_row + li_row
        m_new = jnp.maximum(b_last + m_prev, jnp.max(g_row, axis=-1, keepdims=True))
        decay = jnp.exp(b_last + m_prev - m_new)
        kwt = (k_t[hs, :] * jnp.exp(g_row - m_new)).astype(BF16)
        c_state[h] = decay * c_prev + jnp.dot(kwt, vh, preferred_element_type=F32)
        n_sum = lax.dot_general(ones16, kwt, (((1,), (1,)), ((), ())), preferred_element_type=F32)
        n_state[h] = decay * n_prev + n_sum[0:1, :]
        m_state[h] = m_new

    o_gate = jax.nn.sigmoid(o_ref[0])
    normed = []
    for h in range(MLSTM_HEADS):
        hs = slice(h * MLSTM_DH, (h + 1) * MLSTM_DH)
        hh = outs[h] * o_gate[:, hs]
        mu = jnp.mean(hh, axis=-1, keepdims=True)
        var = jnp.mean(jnp.square(hh - mu), axis=-1, keepdims=True)
        normed.append((hh - mu) * lax.rsqrt(var + EPS))
    hn = jnp.concatenate(normed, axis=-1)
    y_ref[0] = hn * gn_ref[...] + skip_ref[...] * c


def _mlstm(u, vm, o_pre, gates, gates_t, conv_w, conv_b, wq_bd, wkt_bd, gate_bias, skip, gn_g):
    b, l, _ = u.shape
    blk = lambda bi, ci: (bi, ci, 0)
    c2 = lambda bi, ci: (0, 0)
    t512 = pl.BlockSpec((1, BLK, D_MLSTM), blk)
    return pl.pallas_call(
        _mlstm_kernel,
        grid=(b, l // BLK),
        in_specs=[t512, t512, t512, pl.BlockSpec((1, BLK, GATE_LANES), blk),
                  pl.BlockSpec((1, GATE_LANES, BLK), lambda bi, ci: (bi, 0, ci)),
                  pl.BlockSpec((CONV_W, D_MLSTM), c2), pl.BlockSpec((1, D_MLSTM), c2),
                  pl.BlockSpec(wq_bd.shape, c2), pl.BlockSpec(wkt_bd.shape, c2),
                  pl.BlockSpec((1, GATE_LANES), c2), pl.BlockSpec((GATE_LANES, 1), c2),
                  pl.BlockSpec((1, D_MLSTM), c2), pl.BlockSpec((1, D_MLSTM), c2)],
        out_specs=pl.BlockSpec((1, BLK, D_MLSTM), lambda bi, ci: (bi, jnp.maximum(ci - LEAD // BLK, 0), 0)),
        out_shape=jax.ShapeDtypeStruct((b, l - LEAD, D_MLSTM), F32),
        scratch_shapes=[pltpu.VMEM((MLSTM_HEADS, MLSTM_DH, MLSTM_DH), F32),
                        pltpu.VMEM((MLSTM_HEADS, 1, MLSTM_DH), F32),
                        pltpu.VMEM((MLSTM_HEADS, 1, 1), F32),
                        pltpu.VMEM((8, D_MLSTM), F32)],
        compiler_params=pltpu.CompilerParams(dimension_semantics=("parallel", "arbitrary"),
                                             vmem_limit_bytes=VMEM_LIMIT),
        name="mlstm",
    )(u, vm, o_pre, gates, gates_t, conv_w, conv_b, wq_bd, wkt_bd, gate_bias, gate_bias.T, skip, gn_g)


PAIR = 2 * SB_DH
N_PAIRS = SB_HEADS // 2


def _sb_kernel(q_ref, k_ref, v_ref, g_ref, o_ref, run_ref, acc_ref):
    qblk = pl.program_id(1) + LEAD // BLK
    row = lax.broadcasted_iota(jnp.int32, (BLK, BLK), 0)
    col = lax.broadcasted_iota(jnp.int32, (BLK, BLK), 1)
    odd_lane = col >= SB_DH
    jj = lax.broadcasted_iota(jnp.int32, (BLK, 2 * BLK), 0)
    ss = lax.broadcasted_iota(jnp.int32, (BLK, 2 * BLK), 1)
    tri2 = jnp.logical_or(jj > ss, ss >= BLK).astype(BF16)

    qf = q_ref[0].astype(F32) * (SB_DH ** -0.5)
    qm = []
    for h in range(SB_HEADS):
        qp = qf[:, (h // 2) * PAIR:(h // 2 + 1) * PAIR]
        qm.append(jnp.where(odd_lane if h % 2 else jnp.logical_not(odd_lane), qp, 0.0).astype(BF16))

    q_pairs = [jnp.concatenate([qm[2 * p], qm[2 * p + 1]], axis=0) for p in range(N_PAIRS)]

    def key_block(start, mask, first):
        kblk = k_ref[0, pl.ds(start, BLK), :]
        vblk = v_ref[0, pl.ds(start, BLK), :]
        z = jnp.concatenate(
            [lax.dot_general(q_pairs[p], kblk[:, p * PAIR:(p + 1) * PAIR], (((1,), (1,)), ((), ())),
                             preferred_element_type=F32) for p in range(N_PAIRS)], axis=0)
        z = z.reshape(SB_HEADS, BLK, BLK)
        lneg = jnp.where(mask[None], -_softplus(z), 0.0)
        flat = lneg.reshape(SB_HEADS * BLK, BLK)
        hi = flat.astype(BF16)
        lo = (flat - hi.astype(F32)).astype(BF16)
        rt = jnp.dot(hi, tri2, preferred_element_type=F32) + jnp.dot(lo, tri2, preferred_element_type=F32)
        rex = rt[:, :BLK].reshape(SB_HEADS, BLK, BLK)
        tot = rt[:, BLK:].reshape(SB_HEADS, BLK, BLK)
        e = z + lneg + rex
        if not first:
            e = e + run_ref[...]
        a = jnp.where(mask[None], jnp.exp(e), 0.0).astype(BF16)
        run_ref[...] = tot if first else run_ref[...] + tot
        for p in range(N_PAIRS):
            vp = vblk[:, p * PAIR:(p + 1) * PAIR]
            zero = jnp.zeros_like(vp)
            v_heads = jnp.concatenate([jnp.where(odd_lane, zero, vp), jnp.where(odd_lane, vp, zero)], axis=0)
            pv = jnp.dot(jnp.concatenate([a[2 * p], a[2 * p + 1]], axis=1), v_heads,
                         preferred_element_type=F32)
            ps = slice(p * PAIR, (p + 1) * PAIR)
            acc_ref[:, ps] = pv if first else acc_ref[:, ps] + pv

    def alive():
        m = run_ref[0]
        for h in range(1, SB_HEADS):
            m = jnp.maximum(m, run_ref[h])
        return (jnp.max(m) > SB_DEAD_LOG).astype(jnp.int32)

    key_block(pl.multiple_of(qblk * BLK, BLK), col < row, True)

    def cond(carry):
        kb, go = carry
        return jnp.logical_and(kb >= 0, go > 0)

    def body(carry):
        kb, _ = carry
        start = pl.multiple_of(kb * BLK, BLK)
        key_block(start, start + col >= PAD, False)
        return kb - 1, alive()

    lax.while_loop(cond, body, (qblk - 1, alive()))

    for p in range(N_PAIRS):
        ps = slice(p * PAIR, (p + 1) * PAIR)
        x = acc_ref[:, ps]
        x2 = x * x
        ms_even = jnp.sum(jnp.where(odd_lane, 0.0, x2), axis=-1, keepdims=True) * (1.0 / SB_DH)
        ms_odd = jnp.sum(jnp.where(odd_lane, x2, 0.0), axis=-1, keepdims=True) * (1.0 / SB_DH)
        scale = jnp.where(odd_lane, lax.rsqrt(ms_odd + EPS), lax.rsqrt(ms_even + EPS))
        o_ref[0, :, ps] = x * scale * g_ref[:, ps]


def _sb(q, k, v, norm_g):
    b, l, d = q.shape
    nq = (l - LEAD) // BLK
    full = pl.BlockSpec((1, l, d), lambda bi, qi: (bi, 0, 0), pipeline_mode=pl.Buffered(1))
    return pl.pallas_call(
        _sb_kernel,
        grid=(b, nq),
        in_specs=[pl.BlockSpec((1, BLK, d), lambda bi, qi: (bi, qi + LEAD // BLK, 0)),
                  full, full, pl.BlockSpec((1, d), lambda bi, qi: (0, 0))],
        out_specs=pl.BlockSpec((1, BLK, d), lambda bi, qi: (bi, qi, 0)),
        out_shape=jax.ShapeDtypeStruct((b, l - LEAD, d), F32),
        scratch_shapes=[pltpu.VMEM((SB_HEADS, BLK, BLK), F32), pltpu.VMEM((BLK, d), F32)],
        compiler_params=pltpu.CompilerParams(dimension_semantics=("parallel", "arbitrary"),
                                             vmem_limit_bytes=VMEM_LIMIT),
        name="stickbreak",
    )(q, k, v, norm_g)


def _outproj_kernel(x_ref, ym_ref, ys_ref, wm_ref, ws_ref, g_ref, h_ref, nt_ref):
    h = x_ref[0] + jnp.dot(ym_ref[0].astype(BF16), wm_ref[...], preferred_element_type=F32) \
        + jnp.dot(ys_ref[0].astype(BF16), ws_ref[...], preferred_element_type=F32)
    h_ref[0] = h
    nt_ref[...] = (h * _rms_scale(h) * g_ref[...]).T.astype(BF16)


def _outproj(x, ym, ys, w_m, w_s, g, tm=512):
    b, s, d = x.shape
    blk = lambda bi, i: (bi, i, 0)
    c2 = lambda bi, i: (0, 0)
    return pl.pallas_call(
        _outproj_kernel,
        grid=(b, s // tm),
        in_specs=[pl.BlockSpec((1, tm, d), blk), pl.BlockSpec((1, tm, D_MLSTM), blk),
                  pl.BlockSpec((1, tm, D_SB), blk),
                  pl.BlockSpec(w_m.shape, c2), pl.BlockSpec(w_s.shape, c2), pl.BlockSpec((1, d), c2)],
        out_specs=[pl.BlockSpec((1, tm, d), blk), pl.BlockSpec((d, tm), lambda bi, i: (0, bi * (s // tm) + i))],
        out_shape=[jax.ShapeDtypeStruct((b, s, d), F32), jax.ShapeDtypeStruct((d, b * s), BF16)],
        compiler_params=pltpu.CompilerParams(dimension_semantics=("parallel", "parallel"),
                                             vmem_limit_bytes=VMEM_LIMIT),
        name="outproj",
    )(x, ym, ys, w_m, w_s, g)


LOG2E = 1.4426950408889634


def _top_values(s, k):
    vals = []
    for _ in range(k):
        m = jnp.max(s, axis=0, keepdims=True)
        vals.append(m)
        s = jnp.where(s == m, NEG_INF, s)
    return vals


def _exchange(a, i, l, descending):
    hi = jnp.maximum(a[i], a[l])
    lo = jnp.minimum(a[i], a[l])
    a[i], a[l] = (hi, lo) if descending else (lo, hi)


def _bitonic_merge(a):
    j = len(a) // 2
    while j >= 1:
        for i in range(len(a)):
            if i ^ j > i:
                _exchange(a, i, i ^ j, True)
        j //= 2


def _sorted_top16(sc):
    n = PEER_TOPK
    a = [sc[SUBLANES * g:SUBLANES * (g + 1), :] for g in range(n)]
    k = 2
    while k <= n:
        j = k // 2
        while j >= 1:
            for i in range(n):
                if i ^ j > i:
                    _exchange(a, i, i ^ j, (i & k) == 0)
            j //= 2
        k *= 2
    shift = SUBLANES // 2
    while shift >= 1:
        b = [pltpu.roll(x, shift, axis=0) for x in a]
        a = [jnp.maximum(a[i], b[n - 1 - i]) for i in range(n)]
        _bitonic_merge(a)
        shift //= 2
    return [x[0:1, :] for x in a]


def _pair_candidates(top1, top2):
    cand = []
    for r1 in range(PEER_TOPK):
        cand.append(top1[r1] + jnp.concatenate(top2[:PEER_TOPK // (r1 + 1)], axis=0))
    return jnp.concatenate(cand, axis=0)


def _route_kernel(nt_ref, wq_ref, keys_ref, s1_ref, s2_ref, tau_ref):
    qt = jnp.dot(wq_ref[...], nt_ref[...], preferred_element_type=F32)
    for h in range(PEER_HEADS):
        shifted = []
        tops = []
        for p in range(2):
            r0 = (h * 2 + p) * PEER_HALF
            sc = jnp.dot(keys_ref[p], qt[r0:r0 + PEER_HALF, :].astype(BF16), preferred_element_type=F32)
            top = _sorted_top16(sc)
            shifted.append((sc - top[0]) * LOG2E)
            tops.append([(t - top[0]) * LOG2E for t in top])
        cand = _pair_candidates(tops[0], tops[1])
        tau = _top_values(cand, PEER_TOPK)[-1]
        z = jnp.sum(jnp.where(cand >= tau, jnp.exp2(cand), 0.0), axis=0, keepdims=True)
        log2z = jnp.log(z) * LOG2E
        cand = _pair_candidates(tops[0], [t - log2z for t in tops[1]])
        s1_ref[h] = shifted[0]
        s2_ref[h] = shifted[1] - log2z
        tau_ref[h] = _top_values(cand, PEER_TOPK)[-1]


def _route(nt, wq_t, keys16, tb=256):
    m = nt.shape[1]
    big = pl.BlockSpec((PEER_HEADS, N_KEYS, tb), lambda i: (0, 0, i))
    small = pl.BlockSpec((PEER_HEADS, 1, tb), lambda i: (0, 0, i))
    return pl.pallas_call(
        _route_kernel,
        grid=(m // tb,),
        in_specs=[pl.BlockSpec((D_MODEL, tb), lambda i: (0, i)),
                  pl.BlockSpec(wq_t.shape, lambda i: (0, 0)),
                  pl.BlockSpec(keys16.shape, lambda i: (0, 0, 0))],
        out_specs=[big, big, small],
        out_shape=[jax.ShapeDtypeStruct((PEER_HEADS, N_KEYS, m), F32)] * 2
        + [jax.ShapeDtypeStruct((PEER_HEADS, 1, m), F32)],
        compiler_params=pltpu.CompilerParams(dimension_semantics=("parallel",),
                                             vmem_limit_bytes=VMEM_LIMIT),
        name="peer_route",
    )(nt, wq_t, keys16)


def _expert_kernel(nt_ref, u_ref, v_ref, s1_ref, s2_ref, tau_ref, h_ref, g_ref, o_ref, acc_ref, act_ref,
                   wa_ref):
    ei = pl.program_id(1)
    eb, tb = act_ref.shape

    @pl.when(ei == 0)
    def _():
        acc_ref[...] = jnp.zeros_like(acc_ref)

    chunk = 2 * N_KEYS
    n_chunks = eb // chunk

    tok_w = 256
    out_w = 512

    def activations(c, p):
        rows = slice(c * chunk, (c + 1) * chunk)
        cols = slice(p * tok_w, (p + 1) * tok_w)
        a = jnp.dot(u_ref[rows, :], nt_ref[:, cols], preferred_element_type=F32)
        act_ref[rows, cols] = 0.5 * a * (1.0 + lax.erf(a * (2.0 ** -0.5)))

    def gates(r, t):
        rs = slice(r * N_KEYS, (r + 1) * N_KEYS)
        ts = slice(t * 128, (t + 1) * 128)
        gate = None
        for h in range(PEER_HEADS):
            pair = s1_ref[h, r:r + 1, ts] + s2_ref[h, :, ts]
            wgt = jnp.where(pair >= tau_ref[h, :, ts], jnp.exp2(pair), 0.0)
            gate = wgt if gate is None else gate + wgt
        wa_ref[rs, ts] = (gate * act_ref[rs, ts]).astype(BF16)

    def outputs(c, p):
        rows = slice(c * chunk, (c + 1) * chunk)
        cols = slice(p * out_w, (p + 1) * out_w)
        acc_ref[:, cols] += lax.dot_general(wa_ref[rows, :], v_ref[rows, cols], (((0,), (0,)), ((), ())),
                                            preferred_element_type=F32)

    tiles = [(r, t) for r in range(chunk // N_KEYS) for t in range(tb // 128)]
    for stage in range(-1, n_chunks + 1):
        mxu_work = []
        if 0 <= stage + 1 < n_chunks:
            mxu_work += [functools.partial(activations, stage + 1, p) for p in range(tb // tok_w)]
        if 0 <= stage - 1 < n_chunks:
            mxu_work += [functools.partial(outputs, stage - 1, p) for p in range(D_MODEL // out_w)]
        vpu_work = []
        if 0 <= stage < n_chunks:
            vpu_work = [functools.partial(gates, stage * (chunk // N_KEYS) + r, t) for r, t in tiles]
        while mxu_work or vpu_work:
            if vpu_work:
                vpu_work.pop(0)()
            if mxu_work:
                mxu_work.pop(0)()

    @pl.when(ei == pl.num_programs(1) - 1)
    def _():
        h = h_ref[...] + acc_ref[...]
        o_ref[...] = h * _rms_scale(h) * g_ref[...]


def _experts(nt, u16, v16, s1, s2, tau, h2d, g, tb=512, eb=2048):
    m = nt.shape[1]
    ne = u16.shape[0]
    rows = eb // N_KEYS
    tok = lambda ti, ei: (ti, 0)
    exp = lambda ti, ei: (ei, 0)
    return pl.pallas_call(
        _expert_kernel,
        grid=(m // tb, ne // eb),
        in_specs=[pl.BlockSpec((D_MODEL, tb), lambda ti, ei: (0, ti)),
                  pl.BlockSpec((eb, D_MODEL), exp),
                  pl.BlockSpec((eb, D_MODEL), exp),
                  pl.BlockSpec((PEER_HEADS, rows, tb), lambda ti, ei: (0, ei, ti)),
                  pl.BlockSpec((PEER_HEADS, N_KEYS, tb), lambda ti, ei: (0, 0, ti)),
                  pl.BlockSpec((PEER_HEADS, 1, tb), lambda ti, ei: (0, 0, ti)),
                  pl.BlockSpec((tb, D_MODEL), tok),
                  pl.BlockSpec((1, D_MODEL), lambda ti, ei: (0, 0))],
        out_specs=pl.BlockSpec((tb, D_MODEL), tok),
        out_shape=jax.ShapeDtypeStruct((m, D_MODEL), F32),
        scratch_shapes=[pltpu.VMEM((tb, D_MODEL), F32), pltpu.VMEM((eb, tb), F32),
                        pltpu.VMEM((eb, tb), BF16)],
        compiler_params=pltpu.CompilerParams(dimension_semantics=("parallel", "arbitrary"),
                                             vmem_limit_bytes=VMEM_LIMIT),
        name="peer_experts",
    )(nt, u16, v16, s1, s2, tau, h2d, g)


def _block_diag(w):
    h, d, _ = w.shape
    eye = jnp.eye(h, dtype=w.dtype)
    return (eye[:, None, :, None] * w[:, :, None, :]).reshape(h * d, h * d)


def kernel(x, meta, norm_mix_g, w_in, conv_w, conv_b, w_qm, w_km, b_i, b_f, mlstm_skip, mlstm_gn_g,
           sb_norm_g, w_out, norm_ffn_g, w_query, sub_keys, expert_u, expert_v, norm_final_g):
    b, s, d = x.shape
    depth = w_in.shape[0]
    assert depth == 1 and d == D_MODEL and s % 512 == 0
    l = s + LEAD
    lead = jnp.concatenate([jnp.zeros((PAD, d), x.dtype), meta.astype(x.dtype)], axis=0)
    h = jnp.concatenate([jnp.broadcast_to(lead[None], (b, LEAD, d)), x], axis=1)

    w = w_in[0]
    o_u, o_v, o_o = 0, D_MLSTM, 2 * D_MLSTM
    o_i = 3 * D_MLSTM
    o_f = o_i + MLSTM_HEADS
    o_q = o_f + MLSTM_HEADS
    w_main = jnp.concatenate([w[:, o_u:o_i], w[:, o_q:]], axis=1).astype(BF16)
    w_gate = jnp.pad(w[:, o_i:o_q], ((0, 0), (0, GATE_LANES - 2 * MLSTM_HEADS))).astype(BF16)
    gate_bias = jnp.pad(jnp.concatenate([b_i[0], b_f[0]]), (0, GATE_LANES - 2 * MLSTM_HEADS))[None, :]

    u, vm, o_pre, gates, qs, ks, vs = _inproj(h.reshape(b * l, d), norm_mix_g[0][None, :], w_main, w_gate)

    r3 = lambda t: t.reshape(b, l, -1)
    gates3 = r3(gates)
    ym = _mlstm(r3(u), r3(vm), r3(o_pre), gates3, gates3.transpose(0, 2, 1), conv_w[0], conv_b[0][None, :],
                _block_diag(w_qm[0]).astype(BF16), _block_diag(w_km[0]).T.astype(BF16), gate_bias,
                mlstm_skip[0][None, :], mlstm_gn_g[0][None, :])

    ys = _sb(r3(qs), r3(ks), r3(vs), sb_norm_g[0][None, :])

    wo = w_out[0].astype(BF16)
    h2, nt = _outproj(x, ym, ys, wo[:D_MLSTM], wo[D_MLSTM:], norm_ffn_g[0][None, :])

    s1, s2, tau = _route(nt, w_query[0].T.astype(BF16), sub_keys[0].astype(BF16))
    out = _experts(nt, expert_u[0].astype(BF16), expert_v[0].astype(BF16), s1, s2, tau,
                   h2.reshape(b * s, d), norm_final_g[None, :])
    return out.reshape(b, s, d)
```

```python
import functools

import jax
import jax.numpy as jnp
from jax import lax
from jax.experimental import pallas as pl
from jax.experimental.pallas import tpu as pltpu

F32 = jnp.float32
BF16 = jnp.bfloat16

D_MODEL = 1024
D_MLSTM = 512
D_SB = 512
MLSTM_HEADS = 4
MLSTM_DH = 128
SB_HEADS = 8
SB_DH = 64
CONV_W = 4
N_META = 16
LEAD = 128
PAD = LEAD - N_META
EPS = 1e-6

PEER_HEADS = 8
N_KEYS = 128
PEER_TOPK = 16
PEER_HALF = 128

SUBLANES = 8
GATE_LANES = 128
BLK = 128
NEG_INF = float("-inf")
SB_DEAD_LOG = -105.0

VMEM_LIMIT = 56 * 1024 * 1024


def _softplus(z):
    return jnp.maximum(z, 0.0) + jnp.log1p(jnp.exp(-jnp.abs(z)))


def _rms_scale(x):
    return lax.rsqrt(jnp.mean(x * x, axis=-1, keepdims=True) + EPS)


def _inproj_kernel(h_ref, g_ref, wm_ref, wg_ref, u_ref, vm_ref, o_ref, gate_ref, q_ref, k_ref, v_ref):
    x = h_ref[...]
    n = (x * _rms_scale(x) * g_ref[...]).astype(BF16)
    y = jnp.dot(n, wm_ref[...], preferred_element_type=F32)
    u_ref[...] = y[:, 0 * 512:1 * 512]
    vm_ref[...] = y[:, 1 * 512:2 * 512]
    o_ref[...] = y[:, 2 * 512:3 * 512]
    q_ref[...] = y[:, 3 * 512:4 * 512].astype(BF16)
    k_ref[...] = y[:, 4 * 512:5 * 512].astype(BF16)
    v_ref[...] = y[:, 5 * 512:6 * 512].astype(BF16)
    gate_ref[...] = jnp.dot(n, wg_ref[...], preferred_element_type=F32)


def _inproj(h2d, g, w_main, w_gate, tm=256):
    m = h2d.shape[0]
    row = lambda i: (i, 0)
    const = lambda i: (0, 0)
    o512 = pl.BlockSpec((tm, 512), row)
    return pl.pallas_call(
        _inproj_kernel,
        grid=(m // tm,),
        in_specs=[pl.BlockSpec((tm, D_MODEL), row), pl.BlockSpec((1, D_MODEL), const),
                  pl.BlockSpec(w_main.shape, const), pl.BlockSpec(w_gate.shape, const)],
        out_specs=[o512, o512, o512, pl.BlockSpec((tm, GATE_LANES), row), o512, o512, o512],
        out_shape=[jax.ShapeDtypeStruct((m, 512), F32)] * 3
        + [jax.ShapeDtypeStruct((m, GATE_LANES), F32)]
        + [jax.ShapeDtypeStruct((m, 512), BF16)] * 3,
        compiler_params=pltpu.CompilerParams(dimension_semantics=("parallel",),
                                             vmem_limit_bytes=VMEM_LIMIT,
                                             allow_input_fusion=[True, False, False, False]),
        name="inproj",
    )(h2d, g, w_main, w_gate)


def _mlstm_kernel(u_ref, vm_ref, o_ref, gate_ref, gate_t_ref, cw_ref, cb_ref, wq_ref, wkt_ref, gb_ref, gbt_ref,
                  skip_ref, gn_ref, y_ref, c_state, n_state, m_state, u_tail):
    ci = pl.program_id(1)

    @pl.when(ci == 0)
    def _():
        c_state[...] = jnp.zeros_like(c_state)
        n_state[...] = jnp.zeros_like(n_state)
        m_state[...] = jnp.zeros_like(m_state)
        u_tail[...] = jnp.zeros_like(u_tail)

    tpos = ci * BLK + lax.broadcasted_iota(jnp.int32, (BLK, 1), 0)
    valid = tpos >= PAD
    valid_row = ci * BLK + lax.broadcasted_iota(jnp.int32, (1, BLK), 1) >= PAD

    u = jnp.where(valid, u_ref[0], 0.0)
    ext = jnp.concatenate([u_tail[...], u], axis=0)
    u_tail[...] = u[BLK - 8:, :]
    cw = cw_ref[...]
    c = cb_ref[...] + cw[CONV_W - 1:CONV_W, :] * u
    for j in range(CONV_W - 1):
        off = 8 - (CONV_W - 1) + j
        c = c + cw[j:j + 1, :] * ext[off:off + BLK, :]
    c = c * jax.nn.sigmoid(c)
    c16 = c.astype(BF16)

    gpre = gate_ref[0] + gb_ref[...]
    logf = jnp.where(valid, -_softplus(-gpre), 0.0)
    gpre_t = gate_t_ref[0] + gbt_ref[...]
    logi_t = jnp.where(valid_row, gpre_t, NEG_INF)
    logf_t = jnp.where(valid_row, -_softplus(-gpre_t), 0.0)
    r = lax.broadcasted_iota(jnp.int32, (BLK, BLK), 0)
    s = lax.broadcasted_iota(jnp.int32, (BLK, BLK), 1)
    tril = s <= r
    bcum = jnp.dot(tril.astype(F32), logf, preferred_element_type=F32,
                   precision=lax.Precision.HIGHEST)
    bcum_t = jnp.dot(logf_t, (r <= s).astype(F32), preferred_element_type=F32,
                     precision=lax.Precision.HIGHEST)

    q_all = jnp.dot(c16, wq_ref[...], preferred_element_type=F32)
    k_t = lax.dot_general(wkt_ref[...], c16, (((1,), (1,)), ((), ())),
                          preferred_element_type=F32) * (MLSTM_DH ** -0.5)
    q16 = q_all.astype(BF16)
    kt16 = k_t.astype(BF16)
    ones16 = jnp.ones((SUBLANES, BLK), BF16)

    vm = vm_ref[0]
    outs = []
    for h in range(MLSTM_HEADS):
        hs = slice(h * MLSTM_DH, (h + 1) * MLSTM_DH)
        vh = vm[:, hs].astype(BF16)
        b_col = bcum[:, MLSTM_HEADS + h:MLSTM_HEADS + h + 1]
        b_row = bcum_t[MLSTM_HEADS + h:MLSTM_HEADS + h + 1, :]
        li_row = logi_t[h:h + 1, :]
        m_prev = m_state[h]
        c_prev = c_state[h]
        n_prev = n_state[h]

        dmat = jnp.where(tril, b_col - b_row + li_row, NEG_INF)
        inter = b_col + m_prev
        m_t = jnp.maximum(inter, jnp.max(dmat, axis=-1, keepdims=True))
        w_intra = jnp.exp(dmat - m_t)
        w_inter = jnp.exp(inter - m_t)
        sc = jnp.dot(q16[:, hs], kt16[hs, :], preferred_element_type=F32) * w_intra
        num = w_inter * jnp.dot(q16[:, hs], c_prev.astype(BF16), preferred_element_type=F32) \
            + jnp.dot(sc.astype(BF16), vh, preferred_element_type=F32)
        den = w_inter * jnp.sum(q_all[:, hs] * n_prev, axis=-1, keepdims=True) \
            + jnp.sum(sc, axis=-1, keepdims=True)
        outs.append(num / jnp.maximum(jnp.abs(den), jnp.exp(-m_t)))

        b_last = b_row[:, BLK - 1:BLK]
        g_row = b_last - b_row + li_row
        m_new = jnp.maximum(b_last + m_prev, jnp.max(g_row, axis=-1, keepdims=True))
        decay = jnp.exp(b_last + m_prev - m_new)
        kwt = (k_t[hs, :] * jnp.exp(g_row - m_new)).astype(BF16)
        c_state[h] = decay * c_prev + jnp.dot(kwt, vh, preferred_element_type=F32)
        n_sum = lax.dot_general(ones16, kwt, (((1,), (1,)), ((), ())), preferred_element_type=F32)
        n_state[h] = decay * n_prev + n_sum[0:1, :]
        m_state[h] = m_new

    o_gate = jax.nn.sigmoid(o_ref[0])
    normed = []
    for h in range(MLSTM_HEADS):
        hs = slice(h * MLSTM_DH, (h + 1) * MLSTM_DH)
        hh = outs[h] * o_gate[:, hs]
        mu = jnp.mean(hh, axis=-1, keepdims=True)
        var = jnp.mean(jnp.square(hh - mu), axis=-1, keepdims=True)
        normed.append((hh - mu) * lax.rsqrt(var + EPS))
    hn = jnp.concatenate(normed, axis=-1)
    y_ref[0] = hn * gn_ref[...] + skip_ref[...] * c


def _mlstm(u, vm, o_pre, gates, gates_t, conv_w, conv_b, wq_bd, wkt_bd, gate_bias, skip, gn_g):
    b, l, _ = u.shape
    blk = lambda bi, ci: (bi, ci, 0)
    c2 = lambda bi, ci: (0, 0)
    t512 = pl.BlockSpec((1, BLK, D_MLSTM), blk)
    return pl.pallas_call(
        _mlstm_kernel,
        grid=(b, l // BLK),
        in_specs=[t512, t512, t512, pl.BlockSpec((1, BLK, GATE_LANES), blk),
                  pl.BlockSpec((1, GATE_LANES, BLK), lambda bi, ci: (bi, 0, ci)),
                  pl.BlockSpec((CONV_W, D_MLSTM), c2), pl.BlockSpec((1, D_MLSTM), c2),
                  pl.BlockSpec(wq_bd.shape, c2), pl.BlockSpec(wkt_bd.shape, c2),
                  pl.BlockSpec((1, GATE_LANES), c2), pl.BlockSpec((GATE_LANES, 1), c2),
                  pl.BlockSpec((1, D_MLSTM), c2), pl.BlockSpec((1, D_MLSTM), c2)],
        out_specs=pl.BlockSpec((1, BLK, D_MLSTM), lambda bi, ci: (bi, jnp.maximum(ci - LEAD // BLK, 0), 0)),
        out_shape=jax.ShapeDtypeStruct((b, l - LEAD, D_MLSTM), F32),
        scratch_shapes=[pltpu.VMEM((MLSTM_HEADS, MLSTM_DH, MLSTM_DH), F32),
                        pltpu.VMEM((MLSTM_HEADS, 1, MLSTM_DH), F32),
                        pltpu.VMEM((MLSTM_HEADS, 1, 1), F32),
                        pltpu.VMEM((8, D_MLSTM), F32)],
        compiler_params=pltpu.CompilerParams(dimension_semantics=("parallel", "arbitrary"),
                                             vmem_limit_bytes=VMEM_LIMIT),
        name="mlstm",
    )(u, vm, o_pre, gates, gates_t, conv_w, conv_b, wq_bd, wkt_bd, gate_bias, gate_bias.T, skip, gn_g)


PAIR = 2 * SB_DH
N_PAIRS = SB_HEADS // 2


def _sb_kernel(q_ref, k_ref, v_ref, g_ref, o_ref, run_ref, acc_ref):
    qblk = pl.program_id(1) + LEAD // BLK
    row = lax.broadcasted_iota(jnp.int32, (BLK, BLK), 0)
    col = lax.broadcasted_iota(jnp.int32, (BLK, BLK), 1)
    odd_lane = col >= SB_DH
    jj = lax.broadcasted_iota(jnp.int32, (BLK, 2 * BLK), 0)
    ss = lax.broadcasted_iota(jnp.int32, (BLK, 2 * BLK), 1)
    tri2 = jnp.logical_or(jj > ss, ss >= BLK).astype(BF16)

    qf = q_ref[0].astype(F32) * (SB_DH ** -0.5)
    qm = []
    for h in range(SB_HEADS):
        qp = qf[:, (h // 2) * PAIR:(h // 2 + 1) * PAIR]
        qm.append(jnp.where(odd_lane if h % 2 else jnp.logical_not(odd_lane), qp, 0.0).astype(BF16))

    q_pairs = [jnp.concatenate([qm[2 * p], qm[2 * p + 1]], axis=0) for p in range(N_PAIRS)]

    def key_block(start, mask, first):
        kblk = k_ref[0, pl.ds(start, BLK), :]
        vblk = v_ref[0, pl.ds(start, BLK), :]
        z = jnp.concatenate(
            [lax.dot_general(q_pairs[p], kblk[:, p * PAIR:(p + 1) * PAIR], (((1,), (1,)), ((), ())),
                             preferred_element_type=F32) for p in range(N_PAIRS)], axis=0)
        z = z.reshape(SB_HEADS, BLK, BLK)
        lneg = jnp.where(mask[None], -_softplus(z), 0.0)
        flat = lneg.reshape(SB_HEADS * BLK, BLK)
        hi = flat.astype(BF16)
        lo = (flat - hi.astype(F32)).astype(BF16)
        rt = jnp.dot(hi, tri2, preferred_element_type=F32) + jnp.dot(lo, tri2, preferred_element_type=F32)
        rex = rt[:, :BLK].reshape(SB_HEADS, BLK, BLK)
        tot = rt[:, BLK:].reshape(SB_HEADS, BLK, BLK)
        e = z + lneg + rex
        if not first:
            e = e + run_ref[...]
        a = jnp.where(mask[None], jnp.exp(e), 0.0).astype(BF16)
        run_ref[...] = tot if first else run_ref[...] + tot
        for p in range(N_PAIRS):
            vp = vblk[:, p * PAIR:(p + 1) * PAIR]
            zero = jnp.zeros_like(vp)
            v_heads = jnp.concatenate([jnp.where(odd_lane, zero, vp), jnp.where(odd_lane, vp, zero)], axis=0)
            pv = jnp.dot(jnp.concatenate([a[2 * p], a[2 * p + 1]], axis=1), v_heads,
                         preferred_element_type=F32)
            ps = slice(p * PAIR, (p + 1) * PAIR)
            acc_ref[:, ps] = pv if first else acc_ref[:, ps] + pv

    def alive():
        m = run_ref[0]
        for h in range(1, SB_HEADS):
            m = jnp.maximum(m, run_ref[h])
        return (jnp.max(m) > SB_DEAD_LOG).astype(jnp.int32)

    key_block(pl.multiple_of(qblk * BLK, BLK), col < row, True)

    def cond(carry):
        kb, go = carry
        return jnp.logical_and(kb >= 0, go > 0)

    def body(carry):
        kb, _ = carry
        start = pl.multiple_of(kb * BLK, BLK)
        key_block(start, start + col >= PAD, False)
        return kb - 1, alive()

    lax.while_loop(cond, body, (qblk - 1, alive()))

    for p in range(N_PAIRS):
        ps = slice(p * PAIR, (p + 1) * PAIR)
        x = acc_ref[:, ps]
        x2 = x * x
        ms_even = jnp.sum(jnp.where(odd_lane, 0.0, x2), axis=-1, keepdims=True) * (1.0 / SB_DH)
        ms_odd = jnp.sum(jnp.where(odd_lane, x2, 0.0), axis=-1, keepdims=True) * (1.0 / SB_DH)
        scale = jnp.where(odd_lane, lax.rsqrt(ms_odd + EPS), lax.rsqrt(ms_even + EPS))
        o_ref[0, :, ps] = x * scale * g_ref[:, ps]


def _sb(q, k, v, norm_g):
    b, l, d = q.shape
    nq = (l - LEAD) // BLK
    full = pl.BlockSpec((1, l, d), lambda bi, qi: (bi, 0, 0), pipeline_mode=pl.Buffered(1))
    return pl.pallas_call(
        _sb_kernel,
        grid=(b, nq),
        in_specs=[pl.BlockSpec((1, BLK, d), lambda bi, qi: (bi, qi + LEAD // BLK, 0)),
                  full, full, pl.BlockSpec((1, d), lambda bi, qi: (0, 0))],
        out_specs=pl.BlockSpec((1, BLK, d), lambda bi, qi: (bi, qi, 0)),
        out_shape=jax.ShapeDtypeStruct((b, l - LEAD, d), F32),
        scratch_shapes=[pltpu.VMEM((SB_HEADS, BLK, BLK), F32), pltpu.VMEM((BLK, d), F32)],
        compiler_params=pltpu.CompilerParams(dimension_semantics=("parallel", "arbitrary"),
                                             vmem_limit_bytes=VMEM_LIMIT),
        name="stickbreak",
    )(q, k, v, norm_g)


def _outproj_kernel(x_ref, ym_ref, ys_ref, wm_ref, ws_ref, g_ref, h_ref, nt_ref):
    h = x_ref[0] + jnp.dot(ym_ref[0].astype(BF16), wm_ref[...], preferred_element_type=F32) \
        + jnp.dot(ys_ref[0].astype(BF16), ws_ref[...], preferred_element_type=F32)
    h_ref[0] = h
    nt_ref[...] = (h * _rms_scale(h) * g_ref[...]).T.astype(BF16)


def _outproj(x, ym, ys, w_m, w_s, g, tm=512):
    b, s, d = x.shape
    blk = lambda bi, i: (bi, i, 0)
    c2 = lambda bi, i: (0, 0)
    return pl.pallas_call(
        _outproj_kernel,
        grid=(b, s // tm),
        in_specs=[pl.BlockSpec((1, tm, d), blk), pl.BlockSpec((1, tm, D_MLSTM), blk),
                  pl.BlockSpec((1, tm, D_SB), blk),
                  pl.BlockSpec(w_m.shape, c2), pl.BlockSpec(w_s.shape, c2), pl.BlockSpec((1, d), c2)],
        out_specs=[pl.BlockSpec((1, tm, d), blk), pl.BlockSpec((d, tm), lambda bi, i: (0, bi * (s // tm) + i))],
        out_shape=[jax.ShapeDtypeStruct((b, s, d), F32), jax.ShapeDtypeStruct((d, b * s), BF16)],
        compiler_params=pltpu.CompilerParams(dimension_semantics=("parallel", "parallel"),
                                             vmem_limit_bytes=VMEM_LIMIT),
        name="outproj",
    )(x, ym, ys, w_m, w_s, g)


LOG2E = 1.4426950408889634


def _top_values(s, k):
    vals = []
    for _ in range(k):
        m = jnp.max(s, axis=0, keepdims=True)
        vals.append(m)
        s = jnp.where(s == m, NEG_INF, s)
    return vals


def _exchange(a, i, l, descending):
    hi = jnp.maximum(a[i], a[l])
    lo = jnp.minimum(a[i], a[l])
    a[i], a[l] = (hi, lo) if descending else (lo, hi)


def _bitonic_merge(a):
    j = len(a) // 2
    while j >= 1:
        for i in range(len(a)):
            if i ^ j > i:
                _exchange(a, i, i ^ j, True)
        j //= 2


def _sorted_top16(sc):
    n = PEER_TOPK
    a = [sc[SUBLANES * g:SUBLANES * (g + 1), :] for g in range(n)]
    k = 2
    while k <= n:
        j = k // 2
        while j >= 1:
            for i in range(n):
                if i ^ j > i:
                    _exchange(a, i, i ^ j, (i & k) == 0)
            j //= 2
        k *= 2
    shift = SUBLANES // 2
    while shift >= 1:
        b = [pltpu.roll(x, shift, axis=0) for x in a]
        a = [jnp.maximum(a[i], b[n - 1 - i]) for i in range(n)]
        _bitonic_merge(a)
        shift //= 2
    return [x[0:1, :] for x in a]


def _pair_candidates(top1, top2):
    cand = []
    for r1 in range(PEER_TOPK):
        cand.append(top1[r1] + jnp.concatenate(top2[:PEER_TOPK // (r1 + 1)], axis=0))
    return jnp.concatenate(cand, axis=0)


def _route_kernel(nt_ref, wq_ref, keys_ref, s1_ref, s2_ref, tau_ref):
    qt = jnp.dot(wq_ref[...], nt_ref[...], preferred_element_type=F32)
    for h in range(PEER_HEADS):
        shifted = []
        tops = []
        for p in range(2):
            r0 = (h * 2 + p) * PEER_HALF
            sc = jnp.dot(keys_ref[p], qt[r0:r0 + PEER_HALF, :].astype(BF16), preferred_element_type=F32)
            top = _sorted_top16(sc)
            shifted.append((sc - top[0]) * LOG2E)
            tops.append([(t - top[0]) * LOG2E for t in top])
        cand = _pair_candidates(tops[0], tops[1])
        tau = _top_values(cand, PEER_TOPK)[-1]
        z = jnp.sum(jnp.where(cand >= tau, jnp.exp2(cand), 0.0), axis=0, keepdims=True)
        log2z = jnp.log(z) * LOG2E
        cand = _pair_candidates(tops[0], [t - log2z for t in tops[1]])
        s1_ref[h] = shifted[0]
        s2_ref[h] = shifted[1] - log2z
        tau_ref[h] = _top_values(cand, PEER_TOPK)[-1]


def _route(nt, wq_t, keys16, tb=256):
    m = nt.shape[1]
    big = pl.BlockSpec((PEER_HEADS, N_KEYS, tb), lambda i: (0, 0, i))
    small = pl.BlockSpec((PEER_HEADS, 1, tb), lambda i: (0, 0, i))
    return pl.pallas_call(
        _route_kernel,
        grid=(m // tb,),
        in_specs=[pl.BlockSpec((D_MODEL, tb), lambda i: (0, i)),
                  pl.BlockSpec(wq_t.shape, lambda i: (0, 0)),
                  pl.BlockSpec(keys16.shape, lambda i: (0, 0, 0))],
        out_specs=[big, big, small],
        out_shape=[jax.ShapeDtypeStruct((PEER_HEADS, N_KEYS, m), F32)] * 2
        + [jax.ShapeDtypeStruct((PEER_HEADS, 1, m), F32)],
        compiler_params=pltpu.CompilerParams(dimension_semantics=("parallel",),
                                             vmem_limit_bytes=VMEM_LIMIT),
        name="peer_route",
    )(nt, wq_t, keys16)


def _expert_kernel(nt_ref, u_ref, v_ref, s1_ref, s2_ref, tau_ref, h_ref, g_ref, o_ref, acc_ref, act_ref,
                   wa_ref):
    ei = pl.program_id(1)
    eb, tb = act_ref.shape

    @pl.when(ei == 0)
    def _():
        acc_ref[...] = jnp.zeros_like(acc_ref)

    chunk = 2 * N_KEYS
    n_chunks = eb // chunk

    tok_w = 256
    out_w = 512

    def activations(c, p):
        rows = slice(c * chunk, (c + 1) * chunk)
        cols = slice(p * tok_w, (p + 1) * tok_w)
        a = jnp.dot(u_ref[rows, :], nt_ref[:, cols], preferred_element_type=F32)
        act_ref[rows, cols] = 0.5 * a * (1.0 + lax.erf(a * (2.0 ** -0.5)))

    def gates(r, t):
        rs = slice(r * N_KEYS, (r + 1) * N_KEYS)
        ts = slice(t * 128, (t + 1) * 128)
        gate = None
        for h in range(PEER_HEADS):
            pair = s1_ref[h, r:r + 1, ts] + s2_ref[h, :, ts]
            wgt = jnp.where(pair >= tau_ref[h, :, ts], jnp.exp2(pair), 0.0)
            gate = wgt if gate is None else gate + wgt
        wa_ref[rs, ts] = (gate * act_ref[rs, ts]).astype(BF16)

    def outputs(c, p):
        rows = slice(c * chunk, (c + 1) * chunk)
        cols = slice(p * out_w, (p + 1) * out_w)
        acc_ref[:, cols] += lax.dot_general(wa_ref[rows, :], v_ref[rows, cols], (((0,), (0,)), ((), ())),
                                            preferred_element_type=F32)

    tiles = [(r, t) for r in range(chunk // N_KEYS) for t in range(tb // 128)]
    for stage in range(-1, n_chunks + 1):
        mxu_work = []
        if 0 <= stage + 1 < n_chunks:
            mxu_work += [functools.partial(activations, stage + 1, p) for p in range(tb // tok_w)]
        if 0 <= stage - 1 < n_chunks:
            mxu_work += [functools.partial(outputs, stage - 1, p) for p in range(D_MODEL // out_w)]
        vpu_work = []
        if 0 <= stage < n_chunks:
            vpu_work = [functools.partial(gates, stage * (chunk // N_KEYS) + r, t) for r, t in tiles]
        while mxu_work or vpu_work:
            if vpu_work:
                vpu_work.pop(0)()
            if mxu_work:
                mxu_work.pop(0)()

    @pl.when(ei == pl.num_programs(1) - 1)
    def _():
        h = h_ref[...] + acc_ref[...]
        o_ref[...] = h * _rms_scale(h) * g_ref[...]


def _experts(nt, u16, v16, s1, s2, tau, h2d, g, tb=512, eb=2048):
    m = nt.shape[1]
    ne = u16.shape[0]
    rows = eb // N_KEYS
    tok = lambda ti, ei: (ti, 0)
    exp = lambda ti, ei: (ei, 0)
    return pl.pallas_call(
        _expert_kernel,
        grid=(m // tb, ne // eb),
        in_specs=[pl.BlockSpec((D_MODEL, tb), lambda ti, ei: (0, ti)),
                  pl.BlockSpec((eb, D_MODEL), exp),
                  pl.BlockSpec((eb, D_MODEL), exp),
                  pl.BlockSpec((PEER_HEADS, rows, tb), lambda ti, ei: (0, ei, ti)),
                  pl.BlockSpec((PEER_HEADS, N_KEYS, tb), lambda ti, ei: (0, 0, ti)),
                  pl.BlockSpec((PEER_HEADS, 1, tb), lambda ti, ei: (0, 0, ti)),
                  pl.BlockSpec((tb, D_MODEL), tok),
                  pl.BlockSpec((1, D_MODEL), lambda ti, ei: (0, 0))],
        out_specs=pl.BlockSpec((tb, D_MODEL), tok),
        out_shape=jax.ShapeDtypeStruct((m, D_MODEL), F32),
        scratch_shapes=[pltpu.VMEM((tb, D_MODEL), F32), pltpu.VMEM((eb, tb), F32),
                        pltpu.VMEM((eb, tb), BF16)],
        compiler_params=pltpu.CompilerParams(dimension_semantics=("parallel", "arbitrary"),
                                             vmem_limit_bytes=VMEM_LIMIT),
        name="peer_experts",
    )(nt, u16, v16, s1, s2, tau, h2d, g)


def _block_diag(w):
    h, d, _ = w.shape
    eye = jnp.eye(h, dtype=w.dtype)
    return (eye[:, None, :, None] * w[:, :, None, :]).reshape(h * d, h * d)


def kernel(x, meta, norm_mix_g, w_in, conv_w, conv_b, w_qm, w_km, b_i, b_f, mlstm_skip, mlstm_gn_g,
           sb_norm_g, w_out, norm_ffn_g, w_query, sub_keys, expert_u, expert_v, norm_final_g):
    b, s, d = x.shape
    depth = w_in.shape[0]
    assert depth == 1 and d == D_MODEL and s % 512 == 0
    l = s + LEAD
    lead = jnp.concatenate([jnp.zeros((PAD, d), x.dtype), meta.astype(x.dtype)], axis=0)
    h = jnp.concatenate([jnp.broadcast_to(lead[None], (b, LEAD, d)), x], axis=1)

    w = w_in[0]
    o_u, o_v, o_o = 0, D_MLSTM, 2 * D_MLSTM
    o_i = 3 * D_MLSTM
    o_f = o_i + MLSTM_HEADS
    o_q = o_f + MLSTM_HEADS
    w_main = jnp.concatenate([w[:, o_u:o_i], w[:, o_q:]], axis=1).astype(BF16)
    w_gate = jnp.pad(w[:, o_i:o_q], ((0, 0), (0, GATE_LANES - 2 * MLSTM_HEADS))).astype(BF16)
    gate_bias = jnp.pad(jnp.concatenate([b_i[0], b_f[0]]), (0, GATE_LANES - 2 * MLSTM_HEADS))[None, :]

    u, vm, o_pre, gates, qs, ks, vs = _inproj(h.reshape(b * l, d), norm_mix_g[0][None, :], w_main, w_gate)

    r3 = lambda t: t.reshape(b, l, -1)
    gates3 = r3(gates)
    ym = _mlstm(r3(u), r3(vm), r3(o_pre), gates3, gates3.transpose(0, 2, 1), conv_w[0], conv_b[0][None, :],
                _block_diag(w_qm[0]).astype(BF16), _block_diag(w_km[0]).T.astype(BF16), gate_bias,
                mlstm_skip[0][None, :], mlstm_gn_g[0][None, :])

    ys = _sb(r3(qs), r3(ks), r3(vs), sb_norm_g[0][None, :])

    wo = w_out[0].astype(BF16)
    h2, nt = _outproj(x, ym, ys, wo[:D_MLSTM], wo[D_MLSTM:], norm_ffn_g[0][None, :])

    s1, s2, tau = _route(nt, w_query[0].T.astype(BF16), sub_keys[0].astype(BF16))
    out = _experts(nt, expert_u[0].astype(BF16), expert_v[0].astype(BF16), s1, s2, tau,
                   h2.reshape(b * s, d), norm_final_g[None, :])
    return out.reshape(b, s, d)
```
